```python
import math, functools
import jax, jax.numpy as jnp
from jax import lax
import numpy as np

D_MODEL = 1024
BATCH = 4
SEQ = 8192
DEPTH = 2
DEC_BATCH = 128
DEC_SEQ = 4
PAST_LEN = 16384
PAGE_SIZE = 128

A_HEADS = 8
A_NOPE = 64
A_ROPE = 32
A_VDIM = 64
Q_LORA = D_MODEL // 2
KV_LORA = D_MODEL // 4
ROPE_BASE = 10000.0
Q_BLOCK = 128
CONV_DIM = D_MODEL // 2
CONV_W = 31
ML_HEADS = 4
ML_DIM = D_MODEL // 2
ML_HD = ML_DIM // ML_HEADS
ML_CHUNK = 64
ML_FORGET_BIAS = 3.0
MEM_LEN = 256
XA_HEADS = 4
XA_HD = D_MODEL // XA_HEADS
D_FF = 4 * D_MODEL
DN_ALPHA = (2 * DEPTH) ** 0.25
DN_BETA = (8 * DEPTH) ** -0.25
LN_EPS = 1e-5
RMS_EPS = 1e-6
_SPLITS = (Q_LORA, KV_LORA, A_ROPE, 2 * CONV_DIM, ML_DIM, ML_DIM, ML_DIM, ML_DIM, ML_HEADS, ML_HEADS, D_MODEL, D_MODEL, D_MODEL)
IN_COLS = sum(_SPLITS)
F_LO = sum(_SPLITS[:9])

kernel_name = "hybrid_mla_conv_mlstm_decoder_step"

f32 = jnp.float32


def split_cols(z):
    offs = []
    acc = 0
    for w in _SPLITS[:-1]:
        acc += w
        offs.append(acc)
    return jnp.split(z, offs, axis=-1)


def layer_norm(x, g, b):
    xf = x.astype(f32)
    mu = jnp.mean(xf, -1, keepdims=True)
    var = jnp.mean(jnp.square(xf - mu), -1, keepdims=True)
    return ((xf - mu) * lax.rsqrt(var + LN_EPS) * g + b).astype(x.dtype)


def rms_norm(x, g):
    xf = x.astype(f32)
    return (xf * lax.rsqrt(jnp.mean(xf * xf, -1, keepdims=True) + RMS_EPS) * g).astype(x.dtype)


def rope(x, pos):
    half = A_ROPE // 2
    inv = ROPE_BASE ** (-jnp.arange(half, dtype=f32) / half)
    ang = pos.astype(f32)[:, None] * inv[None, :]
    shp = (ang.shape[0],) + (1,) * (x.ndim - 3) + (half,)
    cos = jnp.cos(ang).reshape(shp)
    sin = jnp.sin(ang).reshape(shp)
    x1 = x[..., :half].astype(f32)
    x2 = x[..., half:].astype(f32)
    return jnp.concatenate([x1 * cos - x2 * sin, x2 * cos + x1 * sin], -1).astype(x.dtype)


def mla_project(cq, ckv_raw, kpe_raw, pos, q_norm_g, w_uq, kv_norm_g, w_uk):
    cq = rms_norm(cq, q_norm_g)
    q = jnp.einsum('blr,rhd->blhd', cq, w_uq)
    q_nope, q_pe = q[..., :A_NOPE], q[..., A_NOPE:]
    q_pe = rope(q_pe, pos)
    q_lat = jnp.einsum('blhd,chd->blhc', q_nope, w_uk)
    ckv = rms_norm(ckv_raw, kv_norm_g)
    kpe = rope(kpe_raw, pos)
    return q_lat, q_pe, ckv, kpe


def mla_attend_prompt(q_lat, q_pe, ckv, kpe):
    n, L = q_lat.shape[:2]
    nb = L // Q_BLOCK
    scale = (A_NOPE + A_ROPE) ** -0.5
    qb = q_lat.reshape(n, nb, Q_BLOCK, A_HEADS, KV_LORA).transpose(1, 0, 2, 3, 4)
    pb = q_pe.reshape(n, nb, Q_BLOCK, A_HEADS, A_ROPE).transpose(1, 0, 2, 3, 4)
    kpos = jnp.arange(L)

    def one(args):
        ql, qp, i = args
        s = jnp.einsum('bqhc,bkc->bhqk', ql, ckv) + jnp.einsum('bqhr,bkr->bhqk', qp, kpe)
        s = s.astype(f32) * scale
        qpos = i * Q_BLOCK + jnp.arange(Q_BLOCK)
        s = jnp.where(kpos[None, :] <= qpos[:, None], s, -jnp.inf)
        p = jax.nn.softmax(s, axis=-1).astype(ckv.dtype)
        return jnp.einsum('bhqk,bkc->bqhc', p, ckv)

    o = lax.map(one, (qb, pb, jnp.arange(nb)))
    return o.transpose(1, 0, 2, 3, 4).reshape(n, L, A_HEADS, KV_LORA)


def mla_attend_sample(q_lat, q_pe, ckv_new, kpe_new, pool_ckv, pool_kpe, page_table, layer):
    T = q_lat.shape[1]
    scale = (A_NOPE + A_ROPE) ** -0.5
    mask_new = jnp.arange(T)[None, :] <= jnp.arange(T)[:, None]

    def one(args):
        ql, qp, cn, kn, pages = args
        past_c = pool_ckv[layer, pages].reshape(-1, KV_LORA)
        past_k = pool_kpe[layer, pages].reshape(-1, A_ROPE)
        P = past_c.shape[0]
        s_past = (jnp.einsum('thc,pc->htp', ql, past_c) + jnp.einsum('thr,pr->htp', qp, past_k)).astype(f32) * scale
        s_new = (jnp.einsum('thc,kc->htk', ql, cn) + jnp.einsum('thr,kr->htk', qp, kn)).astype(f32) * scale
        s_new = jnp.where(mask_new[None], s_new, -jnp.inf)
        p = jax.nn.softmax(jnp.concatenate([s_past, s_new], -1), axis=-1).astype(cn.dtype)
        return jnp.einsum('htp,pc->thc', p[..., :P], past_c) + jnp.einsum('htk,kc->thc', p[..., P:], cn)

    return lax.map(one, (q_lat, q_pe, ckv_new, kpe_new, page_table))


def conv_branch(u, buf, conv_w, conv_b, ln_g, ln_b, w_ob):
    a, gte = jnp.split(u, 2, axis=-1)
    g = a * jax.nn.sigmoid(gte)
    xp = jnp.concatenate([buf.astype(g.dtype), g], axis=1)
    y = lax.conv_general_dilated(xp, conv_w[:, None, :].astype(g.dtype), window_strides=(1,), padding='VALID',
                                 dimension_numbers=('NWC', 'WIO', 'NWC'), feature_group_count=CONV_DIM)
    y = jax.nn.silu(layer_norm(y + conv_b, ln_g, ln_b))
    return y @ w_ob, xp[:, -(CONV_W - 1):]


def mlstm_chunk(carry, inp):
    C0, n0, m0 = carry
    q, k, v, ig, lf = inp
    c = q.shape[2]
    b = jnp.cumsum(lf, axis=-1)
    causal = jnp.arange(c)[None, :] <= jnp.arange(c)[:, None]
    D = jnp.where(causal, b[..., :, None] - b[..., None, :] + ig[..., None, :], -jnp.inf)
    inter = b + m0[..., None]
    m = jnp.maximum(inter, jnp.max(D, -1))
    Dw = jnp.exp(D - m[..., None])
    iw = jnp.exp(inter - m)
    Sw = jnp.einsum('nhtd,nhsd->nhts', q, k) * Dw
    num = iw[..., None] * jnp.einsum('nhtd,nhde->nhte', q, C0) + jnp.einsum('nhts,nhse->nhte', Sw, v)
    den = iw * jnp.einsum('nhtd,nhd->nht', q, n0) + jnp.sum(Sw, -1)
    h = num / jnp.maximum(jnp.abs(den), jnp.exp(-m))[..., None]
    mL = m[..., -1]
    wL = jnp.exp(b[..., -1:] - b + ig - mL[..., None])
    decay = jnp.exp(b[..., -1] + m0 - mL)
    C1 = decay[..., None, None] * C0 + jnp.einsum('nhs,nhsd,nhse->nhde', wL, k, v)
    n1 = decay[..., None] * n0 + jnp.einsum('nhs,nhsd->nhd', wL, k)
    return (C1, n1, mL), h


def mlstm_branch(q, k, v, og, ig, fg, C0, n0, m0, ml_norm_g, w_oc):
    n, L = q.shape[:2]
    chunk = math.gcd(L, ML_CHUNK)
    nc = L // chunk

    def heads(t):
        return t.astype(f32).reshape(n, nc, chunk, ML_HEADS, ML_HD).transpose(1, 0, 3, 2, 4)

    def gates(t):
        return t.astype(f32).reshape(n, nc, chunk, ML_HEADS).transpose(1, 0, 3, 2)

    lf = jax.nn.log_sigmoid(fg.astype(f32))
    carry0 = (C0.astype(f32), n0.astype(f32), m0.astype(f32))
    (C1, n1, m1), h = lax.scan(mlstm_chunk, carry0,
                               (heads(q), heads(k * (ML_HD ** -0.5)), heads(v), gates(ig), gates(lf)))
    h = h.transpose(1, 0, 3, 2, 4).reshape(n, L, ML_HEADS, ML_HD)
    mu = jnp.mean(h, -1, keepdims=True)
    var = jnp.mean(jnp.square(h - mu), -1, keepdims=True)
    h = ((h - mu) * lax.rsqrt(var + LN_EPS) * ml_norm_g).reshape(n, L, ML_DIM).astype(q.dtype)
    h = h * jax.nn.sigmoid(og)
    return h @ w_oc, (C1, n1, m1)


def mixer_sublayer(x, pos, attn_fn, conv_buf, C0, n0, m0, w_in, b_in, q_norm_g, w_uq, kv_norm_g, w_uk, w_uv, w_oa,
                   conv_w, conv_b, conv_ln_g, conv_ln_b, w_ob, ml_norm_g, w_oc, w_out):
    n, L = x.shape[:2]
    z = x @ w_in + b_in
    cq, ckv_raw, kpe_raw, conv_u, mq, mk, mv, mo, mi, mf, ga, gb, gc = split_cols(z)
    q_lat, q_pe, ckv, kpe = mla_project(cq, ckv_raw, kpe_raw, pos, q_norm_g, w_uq, kv_norm_g, w_uk)
    o_lat = attn_fn(q_lat, q_pe, ckv, kpe)
    ya = jnp.einsum('blhc,chv->blhv', o_lat, w_uv).reshape(n, L, A_HEADS * A_VDIM) @ w_oa
    yb, new_buf = conv_branch(conv_u, conv_buf, conv_w, conv_b, conv_ln_g, conv_ln_b, w_ob)
    yc, (C1, n1, m1) = mlstm_branch(mq, mk, mv, mo, mi, mf, C0, n0, m0, ml_norm_g, w_oc)
    merged = jax.nn.sigmoid(ga) * ya + jax.nn.sigmoid(gb) * yb + jax.nn.sigmoid(gc) * yc
    return merged @ w_out, (ckv, kpe, new_buf, C1, n1, m1)


def mem_kv(mem, wk, wv):
    n, M = mem.shape[:2]
    return (mem @ wk).reshape(n, M, XA_HEADS, XA_HD), (mem @ wv).reshape(n, M, XA_HEADS, XA_HD)


def cross_attend(x, mk, mv, wq, wo):
    n, L = x.shape[:2]
    q = (x @ wq).reshape(n, L, XA_HEADS, XA_HD)
    s = jnp.einsum('nlhd,nmhd->nhlm', q, mk).astype(f32) * (XA_HD ** -0.5)
    p = jax.nn.softmax(s, axis=-1).astype(x.dtype)
    return jnp.einsum('nhlm,nmhd->nlhd', p, mv).reshape(n, L, D_MODEL) @ wo


def sq_relu_mlp(x, w_up, w_down):
    return jnp.square(jax.nn.relu(x @ w_up)) @ w_down


def setup_inputs(seed: int = 0) -> dict:
    key = jax.random.key(seed)
    ks = iter(jax.random.split(key, 64))

    def nrm(shape, scale=1.0):
        return scale * jax.random.normal(next(ks), shape, f32)

    n_pages = PAST_LEN // PAGE_SIZE
    n_pool = (DEC_BATCH * n_pages * 5) // 4
    perm = jax.random.permutation(next(ks), n_pool)
    page_table = perm[: DEC_BATCH * n_pages].reshape(DEC_BATCH, n_pages).astype(jnp.int32)
    fbias = jnp.zeros((IN_COLS,), f32).at[F_LO:F_LO + ML_HEADS].set(ML_FORGET_BIAS)
    L = DEPTH
    return {
        "x_prompt": nrm((BATCH, SEQ, D_MODEL)),
        "x_sample": nrm((DEC_BATCH, DEC_SEQ, D_MODEL)),
        "mem_prompt": nrm((BATCH, MEM_LEN, D_MODEL)),
        "cache_ckv": nrm((DEPTH, n_pool, PAGE_SIZE, KV_LORA)),
        "cache_kpe": nrm((DEPTH, n_pool, PAGE_SIZE, A_ROPE)),
        "state_conv": nrm((DEPTH, DEC_BATCH, CONV_W - 1, CONV_DIM), 0.5),
        "state_C": nrm((DEPTH, DEC_BATCH, ML_HEADS, ML_HD, ML_HD), 0.5),
        "state_n": nrm((DEPTH, DEC_BATCH, ML_HEADS, ML_HD), 0.5),
        "state_m": nrm((DEPTH, DEC_BATCH, ML_HEADS), 0.5),
        "cache_mem_k": nrm((DEPTH, DEC_BATCH, MEM_LEN, XA_HEADS, XA_HD)),
        "cache_mem_v": nrm((DEPTH, DEC_BATCH, MEM_LEN, XA_HEADS, XA_HD)),
        "page_table": page_table,
        "w_in": nrm((L, D_MODEL, IN_COLS), D_MODEL ** -0.5),
        "b_in": nrm((L, IN_COLS), 0.02) + fbias,
        "q_norm_g": 1.0 + nrm((L, Q_LORA), 0.02),
        "w_uq": nrm((L, Q_LORA, A_HEADS, A_NOPE + A_ROPE), Q_LORA ** -0.5),
        "kv_norm_g": 1.0 + nrm((L, KV_LORA), 0.02),
        "w_uk": nrm((L, KV_LORA, A_HEADS, A_NOPE), KV_LORA ** -0.5),
        "w_uv": nrm((L, KV_LORA, A_HEADS, A_VDIM), KV_LORA ** -0.5),
        "w_oa": nrm((L, A_HEADS * A_VDIM, D_MODEL), (A_HEADS * A_VDIM) ** -0.5),
        "conv_w": nrm((L, CONV_W, CONV_DIM), CONV_W ** -0.5),
        "conv_b": nrm((L, CONV_DIM), 0.02),
        "conv_ln_g": 1.0 + nrm((L, CONV_DIM), 0.02),
        "conv_ln_b": nrm((L, CONV_DIM), 0.02),
        "w_ob": nrm((L, CONV_DIM, D_MODEL), CONV_DIM ** -0.5),
        "ml_norm_g": 1.0 + nrm((L, ML_HEADS, ML_HD), 0.02),
        "w_oc": nrm((L, ML_DIM, D_MODEL), ML_DIM ** -0.5),
        "w_out": nrm((L, D_MODEL, D_MODEL), DN_BETA * D_MODEL ** -0.5),
        "ln1_g": 1.0 + nrm((L, D_MODEL), 0.02),
        "ln1_b": nrm((L, D_MODEL), 0.02),
        "xa_wq": nrm((L, D_MODEL, D_MODEL), D_MODEL ** -0.5),
        "xa_wk": nrm((L, D_MODEL, D_MODEL), D_MODEL ** -0.5),
        "xa_wv": nrm((L, D_MODEL, D_MODEL), D_MODEL ** -0.5),
        "xa_wo": nrm((L, D_MODEL, D_MODEL), DN_BETA * D_MODEL ** -0.5),
        "ln2_g": 1.0 + nrm((L, D_MODEL), 0.02),
        "ln2_b": nrm((L, D_MODEL), 0.02),
        "w_up": nrm((L, D_MODEL, D_FF), D_MODEL ** -0.5),
        "w_down": nrm((L, D_FF, D_MODEL), DN_BETA * D_FF ** -0.5),
        "ln3_g": 1.0 + nrm((L, D_MODEL), 0.02),
        "ln3_b": nrm((L, D_MODEL), 0.02),
    }


def reference(x_prompt, x_sample, mem_prompt, cache_ckv, cache_kpe, state_conv, state_C, state_n, state_m,
              cache_mem_k, cache_mem_v, page_table,
              w_in, b_in, q_norm_g, w_uq, kv_norm_g, w_uk, w_uv, w_oa,
              conv_w, conv_b, conv_ln_g, conv_ln_b, w_ob, ml_norm_g, w_oc, w_out, ln1_g, ln1_b,
              xa_wq, xa_wk, xa_wv, xa_wo, ln2_g, ln2_b, w_up, w_down, ln3_g, ln3_b):
    nP, Lp = x_prompt.shape[:2]
    nS, Ls = x_sample.shape[:2]
    pos_p = jnp.arange(Lp)
    pos_s = PAST_LEN + jnp.arange(Ls)
    xp, xs = x_prompt, x_sample
    ckv_p, kpe_p, conv_p, C_p, n_p, m_p, mk_p, mv_p = [], [], [], [], [], [], [], []
    ckv_s, kpe_s, conv_s, C_s, n_s, m_s = [], [], [], [], [], []
    for l in range(DEPTH):
        mix_w = (w_in[l], b_in[l], q_norm_g[l], w_uq[l], kv_norm_g[l], w_uk[l], w_uv[l], w_oa[l],
                 conv_w[l], conv_b[l], conv_ln_g[l], conv_ln_b[l], w_ob[l], ml_norm_g[l], w_oc[l], w_out[l])
        buf0 = jnp.zeros((nP, CONV_W - 1, CONV_DIM), xp.dtype)
        C0 = jnp.zeros((nP, ML_HEADS, ML_HD, ML_HD), f32)
        n0 = jnp.zeros((nP, ML_HEADS, ML_HD), f32)
        m0 = jnp.zeros((nP, ML_HEADS), f32)
        ymix, (a1, a2, a3, a4, a5, a6) = mixer_sublayer(xp, pos_p, mla_attend_prompt, buf0, C0, n0, m0, *mix_w)
        xp = layer_norm(DN_ALPHA * xp + ymix, ln1_g[l], ln1_b[l])
        mk, mv = mem_kv(mem_prompt, xa_wk[l], xa_wv[l])
        xp = layer_norm(DN_ALPHA * xp + cross_attend(xp, mk, mv, xa_wq[l], xa_wo[l]), ln2_g[l], ln2_b[l])
        xp = layer_norm(DN_ALPHA * xp + sq_relu_mlp(xp, w_up[l], w_down[l]), ln3_g[l], ln3_b[l])
        ckv_p.append(a1); kpe_p.append(a2); conv_p.append(a3); C_p.append(a4); n_p.append(a5); m_p.append(a6)
        mk_p.append(mk); mv_p.append(mv)
        attn_s = functools.partial(mla_attend_sample, pool_ckv=cache_ckv, pool_kpe=cache_kpe,
                                   page_table=page_table, layer=l)
        ymix, (s1, s2, s3, s4, s5, s6) = mixer_sublayer(xs, pos_s, attn_s, state_conv[l], state_C[l], state_n[l],
                                                        state_m[l], *mix_w)
        xs = layer_norm(DN_ALPHA * xs + ymix, ln1_g[l], ln1_b[l])
        xs = layer_norm(DN_ALPHA * xs + cross_attend(xs, cache_mem_k[l], cache_mem_v[l], xa_wq[l], xa_wo[l]),
                        ln2_g[l], ln2_b[l])
        xs = layer_norm(DN_ALPHA * xs + sq_relu_mlp(xs, w_up[l], w_down[l]), ln3_g[l], ln3_b[l])
        ckv_s.append(s1); kpe_s.append(s2); conv_s.append(s3); C_s.append(s4); n_s.append(s5); m_s.append(s6)
    return (xp, xs,
            jnp.stack(ckv_p), jnp.stack(kpe_p), jnp.stack(conv_p), jnp.stack(C_p), jnp.stack(n_p), jnp.stack(m_p),
            jnp.stack(mk_p), jnp.stack(mv_p),
            jnp.stack(ckv_s), jnp.stack(kpe_s), jnp.stack(conv_s), jnp.stack(C_s), jnp.stack(n_s), jnp.stack(m_s))
```

```python
import functools
import math

import jax
import jax.numpy as jnp
from jax import lax
from jax.experimental import pallas as pl
from jax.experimental.pallas import tpu as pltpu

f32 = jnp.float32
bf16 = jnp.bfloat16

LN_EPS = 1e-5
RMS_EPS = 1e-6
ROPE_BASE = 10000.0
LANES = 128
VMEM_LIMIT = 56 * 1024 * 1024
ML_CHUNK_PROMPT = 256
SAMPLE_PAD = 16
NEG_INF = float("-inf")


def _cparams(*sem):
    return pltpu.CompilerParams(dimension_semantics=sem, vmem_limit_bytes=VMEM_LIMIT)


def _tile(n, pref):
    t = min(n, pref)
    while n % t:
        t //= 2
    return t


def _dot(a, b):
    return jnp.dot(a.astype(bf16), b.astype(bf16), preferred_element_type=f32)


def _dot_nt(a, b):
    return lax.dot_general(a.astype(bf16), b.astype(bf16), (((1,), (1,)), ((), ())),
                           preferred_element_type=f32)


def _sigmoid(x):
    return 1.0 / (1.0 + jnp.exp(-x))


def _log_sigmoid(x):
    return jnp.minimum(x, 0.0) - jnp.log(1.0 + jnp.exp(-jnp.abs(x)))


def _layer_norm(x, g, b):
    mu = jnp.mean(x, -1, keepdims=True)
    xc = x - mu
    var = jnp.mean(xc * xc, -1, keepdims=True)
    return xc * lax.rsqrt(var + LN_EPS) * g + b


def _rms_norm(x, g):
    return x * lax.rsqrt(jnp.mean(x * x, -1, keepdims=True) + RMS_EPS) * g


def _full(shape):
    nd = len(shape)
    return pl.BlockSpec(shape, lambda *_: (0,) * nd)


def _inproj_kernel(x_ref, cos_ref, sin_ref, wa_ref, ba_ref, qg_ref, wuq_ref, kg_ref, wuk_ref,
                   wc_ref, bc_ref, wm_ref, bm_ref, wi_ref, bi_ref,
                   q_ref, kv_ref, ckv_ref, kpe_ref, g_ref, mqkv_ref, so_ref, gates_ref,
                   *, ql, kvl, heads, nope, half, cdim, mdim, k_scale):
    x = x_ref[...].astype(bf16)
    cos = cos_ref[...]
    sin = sin_ref[...]
    hp = heads * half

    za = jnp.dot(x, wa_ref[...], preferred_element_type=f32) + ba_ref[...]
    cq = _rms_norm(za[:, :ql], qg_ref[...])
    ckv = _rms_norm(za[:, ql:ql + kvl], kg_ref[...])
    k1 = za[:, ql + kvl:ql + kvl + hp]
    k2 = za[:, ql + kvl + hp:]
    k1r = k1 * cos - k2 * sin
    k2r = k2 * cos + k1 * sin
    ckv_ref[...] = ckv
    kpe_ref[...] = jnp.concatenate([k1r[:, :half], k2r[:, :half]], axis=-1)
    kv_ref[:, :kvl] = ckv.astype(bf16)
    kv_ref[:, kvl:kvl + hp] = k1r.astype(bf16)
    kv_ref[:, kvl + hp:] = k2r.astype(bf16)

    qq = _dot(cq, wuq_ref[...])
    hn = heads * nope
    q1 = qq[:, hn:hn + hp]
    q2 = qq[:, hn + hp:]
    q1r = q1 * cos - q2 * sin
    q2r = q2 * cos + q1 * sin
    lane_head = lax.broadcasted_iota(jnp.int32, (1, hp), 1) // half
    for h in range(heads):
        q_lat = _dot(qq[:, h * nope:(h + 1) * nope], wuk_ref[h])
        q_ref[h, :, :kvl] = q_lat.astype(bf16)
        mine = lane_head == h
        q_ref[h, :, kvl:kvl + hp] = jnp.where(mine, q1r, 0.0).astype(bf16)
        q_ref[h, :, kvl + hp:] = jnp.where(mine, q2r, 0.0).astype(bf16)

    u = jnp.dot(x, wc_ref[...], preferred_element_type=f32) + bc_ref[...]
    g_ref[...] = u[:, :cdim] * _sigmoid(u[:, cdim:])

    zm = jnp.dot(x, wm_ref[...], preferred_element_type=f32) + bm_ref[...]
    mqkv_ref[:, :mdim] = zm[:, :mdim].astype(bf16)
    mqkv_ref[:, mdim:2 * mdim] = (zm[:, mdim:2 * mdim] * k_scale).astype(bf16)
    mqkv_ref[:, 2 * mdim:] = zm[:, 2 * mdim:3 * mdim].astype(bf16)
    so_ref[...] = _sigmoid(zm[:, 3 * mdim:])
    zi = jnp.dot(x, wi_ref[...], preferred_element_type=f32) + bi_ref[...]
    gates_ref[...] = zi[:, :gates_ref.shape[1]]


def _inproj(x, cos_t, sin_t, w, *, dims):
    T, D = x.shape
    tm = _tile(T, 256)
    tab_blocks = cos_t.shape[0] // tm
    H, kvl, ql = dims["heads"], dims["kvl"], dims["ql"]
    hp = H * dims["half"]
    qk = kvl + 2 * hp
    cdim, mdim, ng = dims["cdim"], dims["mdim"], dims["ngates"]
    row = lambda i: (i, 0)
    tab = lambda i: (i % tab_blocks, 0)
    kern = functools.partial(_inproj_kernel, ql=ql, kvl=kvl, heads=H, nope=dims["nope"], half=dims["half"],
                             cdim=cdim, mdim=mdim, k_scale=dims["mhd"] ** -0.5)
    weights = (w["wa"], w["ba"], w["qg"], w["wuq"], w["kg"], w["wuk"], w["wc"], w["bc"], w["wm"], w["bm"],
               w["wi"], w["bi"])
    return pl.pallas_call(
        kern,
        grid=(T // tm,),
        in_specs=[pl.BlockSpec((tm, D), row), pl.BlockSpec((tm, hp), tab), pl.BlockSpec((tm, hp), tab)]
        + [_full(a.shape) for a in weights],
        out_specs=[pl.BlockSpec((H, tm, qk), lambda i: (0, i, 0)), pl.BlockSpec((tm, qk), row),
                   pl.BlockSpec((tm, kvl), row), pl.BlockSpec((tm, 2 * dims["half"]), row),
                   pl.BlockSpec((tm, cdim), row), pl.BlockSpec((tm, 3 * mdim), row),
                   pl.BlockSpec((tm, mdim), row), pl.BlockSpec((tm, ng), row)],
        out_shape=[jax.ShapeDtypeStruct((H, T, qk), bf16), jax.ShapeDtypeStruct((T, qk), bf16),
                   jax.ShapeDtypeStruct((T, kvl), f32), jax.ShapeDtypeStruct((T, 2 * dims["half"]), f32),
                   jax.ShapeDtypeStruct((T, cdim), f32), jax.ShapeDtypeStruct((T, 3 * mdim), bf16),
                   jax.ShapeDtypeStruct((T, mdim), f32), jax.ShapeDtypeStruct((T, ng), f32)],
        compiler_params=_cparams("parallel"),
        name="inproj",
    )(x, cos_t, sin_t, *weights)


def _attn_prompt_kernel(q_ref, kv_ref, o_ref, m_ref, l_ref, acc_ref, *, tq, kvl, scale, hgroup):
    i = pl.program_id(1)
    heads = q_ref.shape[0]
    ngroups = heads // hgroup
    rows = hgroup * tq
    m_ref[...] = jnp.full(m_ref.shape, NEG_INF, f32)
    l_ref[...] = jnp.zeros(l_ref.shape, f32)
    acc_ref[...] = jnp.zeros(acc_ref.shape, f32)

    def block(j, masked):
        start = pl.multiple_of(j * tq, tq)
        k = kv_ref[pl.ds(start, tq), :]
        v = k[:, :kvl]
        for gi in range(ngroups):
            q = q_ref[gi * hgroup:(gi + 1) * hgroup].reshape(rows, q_ref.shape[2])
            s = _dot_nt(q, k) * scale
            if masked:
                qpos = lax.broadcasted_iota(jnp.int32, (rows, tq), 0) % tq
                kpos = lax.broadcasted_iota(jnp.int32, (rows, tq), 1)
                s = jnp.where(kpos <= qpos, s, NEG_INF)
            m_prev = m_ref[gi]
            m_new = jnp.maximum(m_prev, jnp.max(s, -1, keepdims=True))
            alpha = jnp.exp(m_prev - m_new)
            p = jnp.exp(s - m_new)
            l_ref[gi] = alpha * l_ref[gi] + jnp.sum(p, -1, keepdims=True)
            acc_ref[gi] = alpha * acc_ref[gi] + _dot(p, v)
            m_ref[gi] = m_new

    def body(j, carry):
        block(j, False)
        return carry

    lax.fori_loop(0, i, body, 0)
    block(i, True)
    for gi in range(ngroups):
        o = acc_ref[gi] / l_ref[gi]
        o_ref[gi * hgroup:(gi + 1) * hgroup] = o.reshape(hgroup, tq, kvl).astype(o_ref.dtype)


def _attn_prompt(q, kv, n_seq, *, dims):
    H, T, qk = q.shape
    L = T // n_seq
    kvl = dims["kvl"]
    tq = _tile(L, 256)
    nq = L // tq
    hgroup = max(1, min(H, 1024 // tq))
    ngroups = H // hgroup
    kern = functools.partial(_attn_prompt_kernel, tq=tq, kvl=kvl, scale=dims["attn_scale"], hgroup=hgroup)
    return pl.pallas_call(
        kern,
        grid=(n_seq, nq),
        in_specs=[pl.BlockSpec((H, tq, qk), lambda n, i: (0, n * nq + i, 0)),
                  pl.BlockSpec((L, qk), lambda n, i: (n, 0))],
        out_specs=pl.BlockSpec((H, tq, kvl), lambda n, i: (0, n * nq + i, 0)),
        out_shape=jax.ShapeDtypeStruct((H, T, kvl), bf16),
        scratch_shapes=[pltpu.VMEM((ngroups, hgroup * tq, 1), f32), pltpu.VMEM((ngroups, hgroup * tq, 1), f32),
                        pltpu.VMEM((ngroups, hgroup * tq, kvl), f32)],
        compiler_params=_cparams("parallel", "parallel"),
        name="attn_prompt",
    )(q, kv)


def _attn_sample_kernel(pt_ref, q_ref, qpe_ref, kvn_ref, *rest, pages, kvl, rope, scale, n_new):
    ckv_refs = rest[:pages]
    kpe_refs = rest[pages:2 * pages]
    o_ref = rest[2 * pages]
    kbuf, m_ref, l_ref, acc_ref = rest[2 * pages + 1:]
    j = pl.program_id(1)
    psz = ckv_refs[0].shape[0]
    rows = q_ref.shape[0]

    @pl.when(j == 0)
    def _():
        m_ref[...] = jnp.full(m_ref.shape, NEG_INF, f32)
        l_ref[...] = jnp.zeros(l_ref.shape, f32)
        acc_ref[...] = jnp.zeros(acc_ref.shape, f32)
        kbuf[:, kvl + rope:] = jnp.zeros((kbuf.shape[0], kbuf.shape[1] - kvl - rope), bf16)

    def online(s, v):
        m_prev = m_ref[...]
        m_new = jnp.maximum(m_prev, jnp.max(s, -1, keepdims=True))
        alpha = jnp.exp(m_prev - m_new)
        p = jnp.exp(s - m_new)
        l_ref[...] = alpha * l_ref[...] + jnp.sum(p, -1, keepdims=True)
        acc_ref[...] = alpha * acc_ref[...] + _dot(p, v)
        m_ref[...] = m_new

    for p in range(pages):
        kbuf[p * psz:(p + 1) * psz, :kvl] = ckv_refs[p][...].astype(bf16)
        kbuf[p * psz:(p + 1) * psz, kvl:kvl + rope] = kpe_refs[p][...].astype(bf16)
    q_past = jnp.concatenate([q_ref[:, :kvl], qpe_ref[...]], axis=-1)
    k = kbuf[...]
    online(_dot_nt(q_past, k) * scale, k[:, :kvl])

    @pl.when(j == pl.num_programs(1) - 1)
    def _():
        kn = kvn_ref[...]
        s = _dot_nt(q_ref[...], kn) * scale
        tok = lax.broadcasted_iota(jnp.int32, s.shape, 0) % n_new
        col = lax.broadcasted_iota(jnp.int32, s.shape, 1)
        s = jnp.where(col <= tok, s, NEG_INF)
        online(s, kn[:, :kvl])
        o_ref[...] = acc_ref[...] / l_ref[...]


def _attn_sample(page_table, q_s, qpe_s, kv_new, cache_ckv, cache_kpe, layer, *, dims, n_new):
    nb, rows, qk = q_s.shape
    n_pages = page_table.shape[1]
    psz = cache_ckv.shape[2]
    kvl, rope = dims["kvl"], 2 * dims["half"]
    pages = _tile(n_pages, 16)
    steps = n_pages // pages

    def page_map(p):
        return lambda b, j, pt: (layer, pt[b, j * pages + p], 0, 0)

    kern = functools.partial(_attn_sample_kernel, pages=pages, kvl=kvl, rope=rope, scale=dims["attn_scale"],
                             n_new=n_new)
    grid_spec = pltpu.PrefetchScalarGridSpec(
        num_scalar_prefetch=1,
        grid=(nb, steps),
        in_specs=[pl.BlockSpec((None, rows, qk), lambda b, j, pt: (b, 0, 0)),
                  pl.BlockSpec((None, rows, LANES), lambda b, j, pt: (b, 0, 0)),
                  pl.BlockSpec((None, LANES, qk), lambda b, j, pt: (b, 0, 0))]
        + [pl.BlockSpec((None, None, psz, kvl), page_map(p)) for p in range(pages)]
        + [pl.BlockSpec((None, None, psz, rope), page_map(p)) for p in range(pages)],
        out_specs=pl.BlockSpec((None, rows, kvl), lambda b, j, pt: (b, 0, 0)),
        scratch_shapes=[pltpu.VMEM((pages * psz, kvl + LANES), bf16), pltpu.VMEM((rows, 1), f32),
                        pltpu.VMEM((rows, 1), f32), pltpu.VMEM((rows, kvl), f32)],
    )
    return pl.pallas_call(
        kern,
        grid_spec=grid_spec,
        out_shape=jax.ShapeDtypeStruct((nb, rows, kvl), f32),
        compiler_params=_cparams("parallel", "arbitrary"),
        name="attn_sample",
    )(page_table, q_s, qpe_s, kv_new, *([cache_ckv] * pages), *([cache_kpe] * pages))


def _attn_out_kernel(o_ref, wuv_ref, va_ref):
    heads = o_ref.shape[0]
    vd = wuv_ref.shape[2]
    for h in range(heads):
        va_ref[:, h * vd:(h + 1) * vd] = jnp.dot(o_ref[h], wuv_ref[h], preferred_element_type=f32)


def _attn_out(o, wuv):
    H, T, kvl = o.shape
    vd = wuv.shape[2]
    tm = _tile(T, 512)
    return pl.pallas_call(
        _attn_out_kernel,
        grid=(T // tm,),
        in_specs=[pl.BlockSpec((H, tm, kvl), lambda i: (0, i, 0)), _full(wuv.shape)],
        out_specs=pl.BlockSpec((tm, H * vd), lambda i: (i, 0)),
        out_shape=jax.ShapeDtypeStruct((T, H * vd), f32),
        compiler_params=_cparams("parallel"),
        name="attn_out",
    )(o, wuv)


def _conv_prompt_kernel(g_ref, cw_ref, cb_ref, lg_ref, lb_ref, act_ref, buf_ref, xp_ref, *, taps, rc):
    i = pl.program_id(1)
    tm, cdim = g_ref.shape
    head = xp_ref.shape[0] - tm
    keep = taps - 1

    @pl.when(i == 0)
    def _():
        xp_ref[:head, :] = jnp.zeros((head, cdim), f32)

    xp_ref[head:, :] = g_ref[...]
    w = cw_ref[...]
    for c in range(tm // rc):
        acc = jnp.zeros((rc, cdim), f32)
        for t in range(taps):
            off = head - keep + c * rc + t
            acc = acc + xp_ref[off:off + rc, :] * w[t:t + 1, :]
        y = _layer_norm(acc + cb_ref[...], lg_ref[...], lb_ref[...])
        act_ref[c * rc:(c + 1) * rc, :] = (y * _sigmoid(y)).astype(act_ref.dtype)

    @pl.when(i == pl.num_programs(1) - 1)
    def _():
        buf_ref[...] = xp_ref[head + tm - keep:, :]

    xp_ref[:head, :] = xp_ref[tm:, :]


def _conv_prompt(g, n_seq, w):
    T, cdim = g.shape
    L = T // n_seq
    taps = w["conv_w"].shape[0]
    tm = _tile(L, 256)
    nt = L // tm
    head = 32
    assert taps - 1 <= head <= tm
    kern = functools.partial(_conv_prompt_kernel, taps=taps, rc=_tile(tm, 32))
    return pl.pallas_call(
        kern,
        grid=(n_seq, nt),
        in_specs=[pl.BlockSpec((tm, cdim), lambda n, i: (n * nt + i, 0)), _full(w["conv_w"].shape),
                  _full(w["conv_b"].shape), _full(w["conv_ln_g"].shape), _full(w["conv_ln_b"].shape)],
        out_specs=[pl.BlockSpec((tm, cdim), lambda n, i: (n * nt + i, 0)),
                   pl.BlockSpec((None, taps - 1, cdim), lambda n, i: (n, 0, 0))],
        out_shape=[jax.ShapeDtypeStruct((T, cdim), bf16), jax.ShapeDtypeStruct((n_seq, taps - 1, cdim), f32)],
        scratch_shapes=[pltpu.VMEM((head + tm, cdim), f32)],
        compiler_params=_cparams("parallel", "arbitrary"),
        name="conv_prompt",
    )(g, w["conv_w"], w["conv_b"], w["conv_ln_g"], w["conv_ln_b"])


def _conv_sample_kernel(g_ref, buf_ref, cw_ref, cb_ref, lg_ref, lb_ref, act_ref, nbuf_ref, *, taps):
    n_new = g_ref.shape[0]
    keep = taps - 1
    w = cw_ref[...]

    def xp(r):
        return buf_ref[r] if r < keep else g_ref[r - keep]

    for t in range(n_new):
        acc = xp(t) * w[0:1, :]
        for k in range(1, taps):
            acc = acc + xp(t + k) * w[k:k + 1, :]
        y = _layer_norm(acc + cb_ref[...], lg_ref[...], lb_ref[...])
        act_ref[t] = (y * _sigmoid(y)).astype(act_ref.dtype)
    for r in range(keep):
        nbuf_ref[r] = xp(r + n_new)


def _conv_sample(g_t, buf_t, w):
    n_new, nb, cdim = g_t.shape
    taps = w["conv_w"].shape[0]
    sb = _tile(nb, 32)
    kern = functools.partial(_conv_sample_kernel, taps=taps)
    return pl.pallas_call(
        kern,
        grid=(nb // sb,),
        in_specs=[pl.BlockSpec((n_new, sb, cdim), lambda i: (0, i, 0)),
                  pl.BlockSpec((taps - 1, sb, cdim), lambda i: (0, i, 0)), _full(w["conv_w"].shape),
                  _full(w["conv_b"].shape), _full(w["conv_ln_g"].shape), _full(w["conv_ln_b"].shape)],
        out_specs=[pl.BlockSpec((n_new, sb, cdim), lambda i: (0, i, 0)),
                   pl.BlockSpec((taps - 1, sb, cdim), lambda i: (0, i, 0))],
        out_shape=[jax.ShapeDtypeStruct((n_new, nb, cdim), bf16), jax.ShapeDtypeStruct((taps - 1, nb, cdim), f32)],
        compiler_params=_cparams("parallel"),
        name="conv_sample",
    )(g_t, buf_t, w["conv_w"], w["conv_b"], w["conv_ln_g"], w["conv_ln_b"])


def _head_norm_gate(h, g_row, so):
    mu = jnp.mean(h, -1, keepdims=True)
    hc = h - mu
    var = jnp.mean(hc * hc, -1, keepdims=True)
    return hc * lax.rsqrt(var + LN_EPS) * g_row * so


def _mlstm_prompt_kernel(qkv_ref, so_ref, gc_ref, gr_ref, ng_ref, hg_ref, c_out, n_out, m_out,
                         c_s, n_s, m_s, *, heads, hd):
    i = pl.program_id(1)
    c = qkv_ref.shape[0]
    mdim = heads * hd

    @pl.when(i == 0)
    def _():
        c_s[...] = jnp.zeros(c_s.shape, f32)
        n_s[...] = jnp.zeros(n_s.shape, f32)
        m_s[...] = jnp.zeros(m_s.shape, f32)

    row = lax.broadcasted_iota(jnp.int32, (c, c), 0)
    col = lax.broadcasted_iota(jnp.int32, (c, c), 1)
    causal = col <= row
    gcol = gc_ref[...]
    grow = gr_ref[...]
    for h in range(heads):
        q = qkv_ref[:, h * hd:(h + 1) * hd]
        k = qkv_ref[:, mdim + h * hd:mdim + (h + 1) * hd]
        v = qkv_ref[:, 2 * mdim + h * hd:2 * mdim + (h + 1) * hd]
        ig_col = gcol[:, h:h + 1]
        ig_row = grow[h:h + 1, :]
        lf_col = _log_sigmoid(gcol[:, heads + h:heads + h + 1])
        lf_row = _log_sigmoid(grow[heads + h:heads + h + 1, :])
        b_col = jnp.sum(jnp.where(causal, lf_row, 0.0), -1, keepdims=True)
        b_row = jnp.sum(jnp.where(row <= col, lf_col, 0.0), 0, keepdims=True)
        m0 = m_s[h][0:1, 0:1]
        c0 = c_s[h]
        n0 = n_s[h][0:1, :]
        d = jnp.where(causal, b_col - b_row + ig_row, NEG_INF)
        inter = b_col + m0
        m = jnp.maximum(inter, jnp.max(d, -1, keepdims=True))
        dw = jnp.exp(d - m)
        iw = jnp.exp(inter - m)
        sw = _dot_nt(q, k) * dw
        qf = q.astype(f32)
        num = iw * _dot(q, c0) + _dot(sw, v)
        den = iw * jnp.sum(qf * n0, -1, keepdims=True) + jnp.sum(sw, -1, keepdims=True)
        hh = num / jnp.maximum(jnp.abs(den), jnp.exp(-m))
        m_last = m[c - 1:c, :]
        b_last = b_col[c - 1:c, :]
        w_last = jnp.exp(b_last - b_col + ig_col - m_last)
        decay = jnp.exp(b_last + m0 - m_last)
        kw = w_last * k.astype(f32)
        c_s[h] = decay * c0 + _dot(kw.T, v)
        n_s[h] = jnp.broadcast_to(decay * n0 + jnp.sum(kw, 0, keepdims=True), n_s.shape[1:])
        m_s[h] = jnp.broadcast_to(m_last, m_s.shape[1:])
        hg_ref[:, h * hd:(h + 1) * hd] = _head_norm_gate(
            hh, ng_ref[:, h * hd:(h + 1) * hd], so_ref[:, h * hd:(h + 1) * hd]).astype(hg_ref.dtype)

    @pl.when(i == pl.num_programs(1) - 1)
    def _():
        c_out[...] = c_s[...]
        n_out[...] = n_s[...]
        m_out[...] = m_s[...]


def _mlstm_prompt(mqkv, so, gates, n_seq, ng, *, dims):
    T, _ = mqkv.shape
    L = T // n_seq
    heads, hd = dims["mheads"], dims["mhd"]
    mdim = heads * hd
    c = _tile(L, ML_CHUNK_PROMPT)
    nc = L // c
    ngates = gates.shape[1]
    gates_row = gates.reshape(n_seq * nc, c, ngates).transpose(0, 2, 1)
    kern = functools.partial(_mlstm_prompt_kernel, heads=heads, hd=hd)
    rowmap = lambda n, i: (n * nc + i, 0)
    return pl.pallas_call(
        kern,
        grid=(n_seq, nc),
        in_specs=[pl.BlockSpec((c, 3 * mdim), rowmap), pl.BlockSpec((c, mdim), rowmap),
                  pl.BlockSpec((c, ngates), rowmap),
                  pl.BlockSpec((None, ngates, c), lambda n, i: (n * nc + i, 0, 0)), _full(ng.shape)],
        out_specs=[pl.BlockSpec((c, mdim), rowmap),
                   pl.BlockSpec((None, heads, hd, hd), lambda n, i: (n, 0, 0, 0)),
                   pl.BlockSpec((None, heads, 8, hd), lambda n, i: (n, 0, 0, 0)),
                   pl.BlockSpec((None, heads, 8, LANES), lambda n, i: (n, 0, 0, 0))],
        out_shape=[jax.ShapeDtypeStruct((T, mdim), bf16), jax.ShapeDtypeStruct((n_seq, heads, hd, hd), f32),
                   jax.ShapeDtypeStruct((n_seq, heads, 8, hd), f32),
                   jax.ShapeDtypeStruct((n_seq, heads, 8, LANES), f32)],
        scratch_shapes=[pltpu.VMEM((heads, hd, hd), f32), pltpu.VMEM((heads, 8, hd), f32),
                        pltpu.VMEM((heads, 8, LANES), f32)],
        compiler_params=_cparams("parallel", "arbitrary"),
        name="mlstm_prompt",
    )(mqkv, so, gates, gates_row, ng)


def _mlstm_sample_kernel(qkv_ref, so_ref, gc_ref, gr_ref, c0_ref, n0_ref, m0_ref, ng_ref,
                         hg_ref, c_out, n_out, m_out, *, heads, hd, n_new):
    sb, c, _ = qkv_ref.shape
    mdim = heads * hd
    row = lax.broadcasted_iota(jnp.int32, (c, c), 0)
    col = lax.broadcasted_iota(jnp.int32, (c, c), 1)
    causal = col <= row
    valid_col = lax.broadcasted_iota(jnp.int32, (c, 1), 0) < n_new
    valid_row = lax.broadcasted_iota(jnp.int32, (1, c), 1) < n_new

    def seq(b, carry):
        gcol = gc_ref[b]
        grow = gr_ref[b]
        for h in range(heads):
            q = qkv_ref[b, :, h * hd:(h + 1) * hd]
            kf = qkv_ref[b, :, mdim + h * hd:mdim + (h + 1) * hd].astype(f32)
            vf = qkv_ref[b, :, 2 * mdim + h * hd:2 * mdim + (h + 1) * hd].astype(f32)
            qf = q.astype(f32)
            ig_col = jnp.where(valid_col, gcol[:, h:h + 1], NEG_INF)
            ig_row = jnp.where(valid_row, grow[h:h + 1, :], NEG_INF)
            lf_col = jnp.where(valid_col, _log_sigmoid(gcol[:, heads + h:heads + h + 1]), 0.0)
            lf_row = jnp.where(valid_row, _log_sigmoid(grow[heads + h:heads + h + 1, :]), 0.0)
            b_col = jnp.sum(jnp.where(causal, lf_row, 0.0), -1, keepdims=True)
            b_row = jnp.sum(jnp.where(row <= col, lf_col, 0.0), 0, keepdims=True)
            m0 = m0_ref[b, h:h + 1, 0:1]
            c0 = c0_ref[b, h]
            n0 = n0_ref[b, h:h + 1, :]
            d = jnp.where(causal, b_col - b_row + ig_row, NEG_INF)
            inter = b_col + m0
            m = jnp.maximum(inter, jnp.max(d, -1, keepdims=True))
            dw = jnp.exp(d - m)
            iw = jnp.exp(inter - m)
            num = iw * _dot(q, c0)
            den = iw * jnp.sum(qf * n0, -1, keepdims=True)
            for s in range(n_new):
                sw_s = jnp.sum(qf * kf[s:s + 1, :], -1, keepdims=True) * dw[:, s:s + 1]
                num = num + sw_s * vf[s:s + 1, :]
                den = den + sw_s
            hh = num / jnp.maximum(jnp.abs(den), jnp.exp(-m))
            m_last = m[n_new - 1:n_new, :]
            b_last = b_col[n_new - 1:n_new, :]
            w_last = jnp.exp(b_last - b_col + ig_col - m_last)
            decay = jnp.exp(b_last + m0 - m_last)
            kw = w_last * kf
            kwt = kw.T
            c1 = decay * c0
            for s in range(n_new):
                c1 = c1 + kwt[:, s:s + 1] * vf[s:s + 1, :]
            c_out[b, h] = c1
            n_out[b, h:h + 1, :] = decay * n0 + jnp.sum(kw, 0, keepdims=True)
            m_out[b, h:h + 1, :] = jnp.broadcast_to(m_last, (1, m_out.shape[2]))
            hg_ref[b, :, h * hd:(h + 1) * hd] = _head_norm_gate(
                hh, ng_ref[:, h * hd:(h + 1) * hd], so_ref[b, :, h * hd:(h + 1) * hd]).astype(hg_ref.dtype)
        return carry

    lax.fori_loop(0, sb, seq, 0)


def _mlstm_sample(mqkv_p, so_p, gcol_p, grow_p, state_c, n0, m0b, layer, ng, *, dims, n_new):
    nb, c, _ = mqkv_p.shape
    heads, hd = dims["mheads"], dims["mhd"]
    mdim = heads * hd
    ngates = gcol_p.shape[2]
    sb = _tile(nb, 8)
    kern = functools.partial(_mlstm_sample_kernel, heads=heads, hd=hd, n_new=n_new)
    seqmap = lambda i: (i, 0, 0)
    return pl.pallas_call(
        kern,
        grid=(nb // sb,),
        in_specs=[pl.BlockSpec((sb, c, 3 * mdim), seqmap), pl.BlockSpec((sb, c, mdim), seqmap),
                  pl.BlockSpec((sb, c, ngates), seqmap), pl.BlockSpec((sb, ngates, c), seqmap),
                  pl.BlockSpec((None, sb, heads, hd, hd), lambda i: (layer, i, 0, 0, 0)),
                  pl.BlockSpec((sb, heads, hd), seqmap), pl.BlockSpec((sb, heads, LANES), seqmap),
                  _full(ng.shape)],
        out_specs=[pl.BlockSpec((sb, c, mdim), seqmap), pl.BlockSpec((sb, heads, hd, hd), lambda i: (i, 0, 0, 0)),
                   pl.BlockSpec((sb, heads, hd), seqmap), pl.BlockSpec((sb, heads, LANES), seqmap)],
        out_shape=[jax.ShapeDtypeStruct((nb, c, mdim), bf16), jax.ShapeDtypeStruct((nb, heads, hd, hd), f32),
                   jax.ShapeDtypeStruct((nb, heads, hd), f32), jax.ShapeDtypeStruct((nb, heads, LANES), f32)],
        compiler_params=_cparams("parallel"),
        name="mlstm_sample",
    )(mqkv_p, so_p, gcol_p, grow_p, state_c, n0, m0b, ng)


def _merge_kernel(x_ref, va_ref, cb_ref, hc_ref, wg_ref, bg_ref, woa_ref, wob_ref, woc_ref, wout_ref,
                  lg_ref, lb_ref, wq_ref, x1_ref, qx_ref, *, alpha):
    x = x_ref[...]
    xb = x.astype(bf16)
    d = x.shape[1]
    merged = jnp.zeros(x.shape, f32)
    for k, (b_ref, w_ref) in enumerate(((va_ref, woa_ref), (cb_ref, wob_ref), (hc_ref, woc_ref))):
        gate = _sigmoid(jnp.dot(xb, wg_ref[:, k * d:(k + 1) * d], preferred_element_type=f32)
                        + bg_ref[:, k * d:(k + 1) * d])
        merged = merged + gate * _dot(b_ref[...], w_ref[...])
    mix = _dot(merged, wout_ref[...])
    x1 = _layer_norm(alpha * x + mix, lg_ref[...], lb_ref[...])
    x1_ref[...] = x1
    qx_ref[...] = _dot(x1, wq_ref[...]).astype(qx_ref.dtype)


def _merge(x, va, cb, hc, w, *, alpha):
    T, D = x.shape
    tm = _tile(T, 256)
    row = lambda i: (i, 0)
    weights = (w["wg"], w["bg"], w["w_oa"], w["w_ob"], w["w_oc"], w["w_out"], w["ln1_g"], w["ln1_b"], w["xa_wq"])
    return pl.pallas_call(
        functools.partial(_merge_kernel, alpha=alpha),
        grid=(T // tm,),
        in_specs=[pl.BlockSpec((tm, D), row), pl.BlockSpec((tm, va.shape[1]), row),
                  pl.BlockSpec((tm, cb.shape[1]), row), pl.BlockSpec((tm, hc.shape[1]), row)]
        + [_full(a.shape) for a in weights],
        out_specs=[pl.BlockSpec((tm, D), row), pl.BlockSpec((tm, D), row)],
        out_shape=[jax.ShapeDtypeStruct((T, D), f32), jax.ShapeDtypeStruct((T, D), bf16)],
        compiler_params=_cparams("parallel"),
        name="merge",
    )(x, va, cb, hc, *weights)


def _mem_kv_kernel(mem_ref, wk_ref, wv_ref, mk_ref, mv_ref):
    mem = mem_ref[...].astype(bf16)
    mk_ref[...] = jnp.dot(mem, wk_ref[...], preferred_element_type=f32)
    mv_ref[...] = jnp.dot(mem, wv_ref[...], preferred_element_type=f32)


def _mem_kv(mem, wk, wv):
    R, D = mem.shape
    tm = _tile(R, 256)
    row = lambda i: (i, 0)
    return pl.pallas_call(
        _mem_kv_kernel,
        grid=(R // tm,),
        in_specs=[pl.BlockSpec((tm, D), row), _full(wk.shape), _full(wv.shape)],
        out_specs=[pl.BlockSpec((tm, D), row), pl.BlockSpec((tm, D), row)],
        out_shape=[jax.ShapeDtypeStruct((R, D), f32), jax.ShapeDtypeStruct((R, D), f32)],
        compiler_params=_cparams("parallel"),
        name="mem_kv",
    )(mem, wk, wv)


def _xattn_heads(q, mk, mv, heads, hd, scale):
    outs = []
    for h in range(heads):
        s = _dot_nt(q[:, h * hd:(h + 1) * hd], mk[:, h * hd:(h + 1) * hd]) * scale
        e = jnp.exp(s - jnp.max(s, -1, keepdims=True))
        p = e / jnp.sum(e, -1, keepdims=True)
        outs.append(_dot(p, mv[:, h * hd:(h + 1) * hd]))
    return outs


def _xattn_prompt_kernel(q_ref, mk_ref, mv_ref, ctx_ref, *, heads, hd, scale):
    outs = _xattn_heads(q_ref[...], mk_ref[...], mv_ref[...], heads, hd, scale)
    for h in range(heads):
        ctx_ref[:, h * hd:(h + 1) * hd] = outs[h].astype(ctx_ref.dtype)


def _xattn_prompt(qx, mk, mv, n_seq, *, dims):
    T, D = qx.shape
    L = T // n_seq
    M = mk.shape[0] // n_seq
    tm = _tile(L, 512)
    nt = L // tm
    heads, hd = dims["xheads"], dims["xhd"]
    kern = functools.partial(_xattn_prompt_kernel, heads=heads, hd=hd, scale=hd ** -0.5)
    return pl.pallas_call(
        kern,
        grid=(n_seq, nt),
        in_specs=[pl.BlockSpec((tm, D), lambda n, i: (n * nt + i, 0)), pl.BlockSpec((M, D), lambda n, i: (n, 0)),
                  pl.BlockSpec((M, D), lambda n, i: (n, 0))],
        out_specs=pl.BlockSpec((tm, D), lambda n, i: (n * nt + i, 0)),
        out_shape=jax.ShapeDtypeStruct((T, D), bf16),
        compiler_params=_cparams("parallel", "parallel"),
        name="xattn_prompt",
    )(qx, mk, mv)


def _xattn_sample_kernel(q_ref, mk_ref, mv_ref, ctx_ref, *, heads, hd, scale):
    sb = q_ref.shape[0]

    def seq(b, carry):
        outs = _xattn_heads(q_ref[b], mk_ref[b], mv_ref[b], heads, hd, scale)
        for h in range(heads):
            ctx_ref[b, :, h * hd:(h + 1) * hd] = outs[h].astype(ctx_ref.dtype)
        return carry

    lax.fori_loop(0, sb, seq, 0)


def _xattn_sample(qx_p, mem_k, mem_v, layer, *, dims):
    nb, c, D = qx_p.shape
    M = mem_k.shape[2]
    sb = _tile(nb, 4)
    heads, hd = dims["xheads"], dims["xhd"]
    kern = functools.partial(_xattn_sample_kernel, heads=heads, hd=hd, scale=hd ** -0.5)
    memmap = lambda i: (layer, i, 0, 0)
    return pl.pallas_call(
        kern,
        grid=(nb // sb,),
        in_specs=[pl.BlockSpec((sb, c, D), lambda i: (i, 0, 0)), pl.BlockSpec((None, sb, M, D), memmap),
                  pl.BlockSpec((None, sb, M, D), memmap)],
        out_specs=pl.BlockSpec((sb, c, D), lambda i: (i, 0, 0)),
        out_shape=jax.ShapeDtypeStruct((nb, c, D), bf16),
        compiler_params=_cparams("parallel"),
        name="xattn_sample",
    )(qx_p, mem_k, mem_v)


def _post_kernel(ctx_ref, x1_ref, wo_ref, l2g_ref, l2b_ref, wup_ref, wdn_ref, l3g_ref, l3b_ref, x3_ref,
                 *, alpha, ff_chunk):
    x1 = x1_ref[...]
    x2 = _layer_norm(alpha * x1 + jnp.dot(ctx_ref[...], wo_ref[...], preferred_element_type=f32),
                     l2g_ref[...], l2b_ref[...])
    x2b = x2.astype(bf16)
    dff = wup_ref.shape[1]
    acc = jnp.zeros(x1.shape, f32)
    for c in range(dff // ff_chunk):
        hcol = jnp.maximum(jnp.dot(x2b, wup_ref[:, c * ff_chunk:(c + 1) * ff_chunk],
                                   preferred_element_type=f32), 0.0)
        acc = acc + _dot(hcol * hcol, wdn_ref[c * ff_chunk:(c + 1) * ff_chunk, :])
    x3_ref[...] = _layer_norm(alpha * x2 + acc, l3g_ref[...], l3b_ref[...])


def _post(ctx, x1, w, *, alpha):
    T, D = x1.shape
    tm = _tile(T, 256)
    row = lambda i: (i, 0)
    weights = (w["xa_wo"], w["ln2_g"], w["ln2_b"], w["w_up"], w["w_down"], w["ln3_g"], w["ln3_b"])
    return pl.pallas_call(
        functools.partial(_post_kernel, alpha=alpha, ff_chunk=_tile(w["w_up"].shape[1], 1024)),
        grid=(T // tm,),
        in_specs=[pl.BlockSpec((tm, D), row), pl.BlockSpec((tm, D), row)] + [_full(a.shape) for a in weights],
        out_specs=pl.BlockSpec((tm, D), row),
        out_shape=jax.ShapeDtypeStruct((T, D), f32),
        compiler_params=_cparams("parallel"),
        name="post",
    )(ctx, x1, *weights)


def _rope_tables(pos, half, heads):
    inv = ROPE_BASE ** (-jnp.arange(half, dtype=f32) / half)
    ang = pos.astype(f32)[:, None] * inv[None, :]
    return jnp.tile(jnp.cos(ang), (1, heads)), jnp.tile(jnp.sin(ang), (1, heads))


def _prep_layer(l, p, dims):
    ql, kvl, half, H, nope = dims["ql"], dims["kvl"], dims["half"], dims["heads"], dims["nope"]
    cdim, mdim, mheads = dims["cdim"], dims["mdim"], dims["mheads"]
    D = dims["d"]
    w_in, b_in = p["w_in"][l], p["b_in"][l]
    o = 0

    def take(n):
        nonlocal o
        cols = (w_in[:, o:o + n], b_in[o:o + n])
        o += n
        return cols

    (w_cq, b_cq), (w_kv, b_kv), (w_kp, b_kp) = take(ql), take(kvl), take(2 * half)
    (w_cu, b_cu) = take(2 * cdim)
    (w_m, b_m) = take(4 * mdim)
    (w_if, b_if) = take(2 * mheads)
    (w_g, b_g) = take(3 * D)
    tile_h = lambda a: jnp.tile(a, (1, H)) if a.ndim == 2 else jnp.tile(a, H)
    wa = jnp.concatenate([w_cq, w_kv, tile_h(w_kp[:, :half]), tile_h(w_kp[:, half:])], 1)
    ba = jnp.concatenate([b_cq, b_kv, tile_h(b_kp[:half]), tile_h(b_kp[half:])])
    w_uq = p["w_uq"][l]
    wuq = jnp.concatenate([w_uq[:, :, :nope].reshape(ql, H * nope),
                           w_uq[:, :, nope:nope + half].reshape(ql, H * half),
                           w_uq[:, :, nope + half:].reshape(ql, H * half)], 1)
    pad_if = LANES - 2 * mheads
    return {
        "wa": wa.astype(bf16), "ba": ba[None], "qg": p["q_norm_g"][l][None], "wuq": wuq.astype(bf16),
        "kg": p["kv_norm_g"][l][None], "wuk": p["w_uk"][l].transpose(1, 2, 0).astype(bf16),
        "wc": w_cu.astype(bf16), "bc": b_cu[None], "wm": w_m.astype(bf16), "bm": b_m[None],
        "wi": jnp.pad(w_if, ((0, 0), (0, pad_if))).astype(bf16), "bi": jnp.pad(b_if, (0, pad_if))[None],
        "wuv": p["w_uv"][l].transpose(1, 0, 2).astype(bf16),
        "conv_w": p["conv_w"][l], "conv_b": p["conv_b"][l][None], "conv_ln_g": p["conv_ln_g"][l][None],
        "conv_ln_b": p["conv_ln_b"][l][None],
        "ml_norm_g": p["ml_norm_g"][l].reshape(1, mdim),
        "wg": w_g.astype(bf16), "bg": b_g[None], "w_oa": p["w_oa"][l].astype(bf16),
        "w_ob": p["w_ob"][l].astype(bf16), "w_oc": p["w_oc"][l].astype(bf16), "w_out": p["w_out"][l].astype(bf16),
        "ln1_g": p["ln1_g"][l][None], "ln1_b": p["ln1_b"][l][None], "xa_wq": p["xa_wq"][l].astype(bf16),
        "xa_wk": p["xa_wk"][l].astype(bf16), "xa_wv": p["xa_wv"][l].astype(bf16),
        "xa_wo": p["xa_wo"][l].astype(bf16), "ln2_g": p["ln2_g"][l][None], "ln2_b": p["ln2_b"][l][None],
        "w_up": p["w_up"][l].astype(bf16), "w_down": p["w_down"][l].astype(bf16),
        "ln3_g": p["ln3_g"][l][None], "ln3_b": p["ln3_b"][l][None],
    }


def _pad_tokens(a, c):
    return jnp.pad(a, ((0, 0), (0, c - a.shape[1])) + ((0, 0),) * (a.ndim - 2))


def kernel(x_prompt, x_sample, mem_prompt, cache_ckv, cache_kpe, state_conv, state_C, state_n, state_m, cache_mem_k, cache_mem_v, page_table, w_in, b_in, q_norm_g, w_uq, kv_norm_g, w_uk, w_uv, w_oa, conv_w, conv_b, conv_ln_g, conv_ln_b, w_ob, ml_norm_g, w_oc, w_out, ln1_g, ln1_b, xa_wq, xa_wk, xa_wv, xa_wo, ln2_g, ln2_b, w_up, w_down, ln3_g, ln3_b):
    params = dict(w_in=w_in, b_in=b_in, q_norm_g=q_norm_g, w_uq=w_uq, kv_norm_g=kv_norm_g, w_uk=w_uk, w_uv=w_uv,
                  w_oa=w_oa, conv_w=conv_w, conv_b=conv_b, conv_ln_g=conv_ln_g, conv_ln_b=conv_ln_b, w_ob=w_ob,
                  ml_norm_g=ml_norm_g, w_oc=w_oc, w_out=w_out, ln1_g=ln1_g, ln1_b=ln1_b, xa_wq=xa_wq, xa_wk=xa_wk,
                  xa_wv=xa_wv, xa_wo=xa_wo, ln2_g=ln2_g, ln2_b=ln2_b, w_up=w_up, w_down=w_down, ln3_g=ln3_g,
                  ln3_b=ln3_b)
    depth, D, _ = w_in.shape
    nP, Lp, _ = x_prompt.shape
    nS, Ls, _ = x_sample.shape
    H, nope = w_uk.shape[2], w_uk.shape[3]
    rope = cache_kpe.shape[-1]
    half = rope // 2
    mheads, mhd = ml_norm_g.shape[1], ml_norm_g.shape[2]
    xheads, xhd = cache_mem_k.shape[-2], cache_mem_k.shape[-1]
    M = mem_prompt.shape[1]
    dims = dict(d=D, ql=q_norm_g.shape[1], kvl=kv_norm_g.shape[1], heads=H, nope=nope, half=half,
                cdim=conv_w.shape[2], mdim=mheads * mhd, mheads=mheads, mhd=mhd, ngates=2 * mheads,
                xheads=xheads, xhd=xhd, attn_scale=(nope + rope) ** -0.5)
    assert H * half == LANES and 2 * mheads <= LANES
    kvl, mdim, cdim, taps = dims["kvl"], dims["mdim"], dims["cdim"], conv_w.shape[1]
    alpha = (2 * depth) ** 0.25
    past_len = page_table.shape[1] * cache_ckv.shape[2]
    cos_p, sin_p = _rope_tables(jnp.arange(Lp), half, H)
    cos_s, sin_s = _rope_tables(jnp.tile(past_len + jnp.arange(Ls), nS), half, H)
    C = SAMPLE_PAD

    xp = x_prompt.reshape(nP * Lp, D)
    xs = x_sample.reshape(nS * Ls, D)
    memf = mem_prompt.reshape(nP * M, D)
    mem_k = cache_mem_k.reshape(depth, nS, M, D)
    mem_v = cache_mem_v.reshape(depth, nS, M, D)
    outs_p = [[] for _ in range(8)]
    outs_s = [[] for _ in range(6)]
    for l in range(depth):
        w = _prep_layer(l, params, dims)
        q, kv, ckv, kpe, g, mqkv, so, gates = _inproj(xp, cos_p, sin_p, w, dims=dims)
        va = _attn_out(_attn_prompt(q, kv, nP, dims=dims), w["wuv"]).astype(bf16)
        cb, nbuf = _conv_prompt(g, nP, w)
        hc, c1, n1, m1 = _mlstm_prompt(mqkv, so, gates, nP, w["ml_norm_g"], dims=dims)
        x1, qx = _merge(xp, va, cb, hc, w, alpha=alpha)
        mk, mv = _mem_kv(memf, w["xa_wk"], w["xa_wv"])
        ctx = _xattn_prompt(qx, mk, mv, nP, dims=dims)
        xp = _post(ctx, x1, w, alpha=alpha)
        for lst, val in zip(outs_p, (ckv.reshape(nP, Lp, kvl), kpe.reshape(nP, Lp, rope), nbuf, c1,
                                     n1[:, :, 0, :], m1[:, :, 0, 0], mk.reshape(nP, M, xheads, xhd),
                                     mv.reshape(nP, M, xheads, xhd))):
            lst.append(val)
        q, kv, ckv, kpe, g, mqkv, so, gates = _inproj(xs, cos_s, sin_s, w, dims=dims)
        q_s = q.reshape(H, nS, Ls, -1).transpose(1, 0, 2, 3).reshape(nS, H * Ls, -1)
        qpe = q_s[:, :, kvl:].reshape(nS, H, Ls, 2, H, half)
        qpe = qpe[:, jnp.arange(H), :, :, jnp.arange(H), :]
        qpe = qpe.transpose(1, 0, 2, 3, 4).reshape(nS, H * Ls, rope)
        qpe = jnp.pad(qpe, ((0, 0), (0, 0), (0, LANES - rope)))
        kv_new = jnp.pad(kv.reshape(nS, Ls, -1), ((0, 0), (0, LANES - Ls), (0, 0)))
        o_s = _attn_sample(page_table, q_s, qpe, kv_new, cache_ckv, cache_kpe, l, dims=dims, n_new=Ls)
        o_s = o_s.reshape(nS, H, Ls, kvl).transpose(1, 0, 2, 3).reshape(H, nS * Ls, kvl).astype(bf16)
        va = _attn_out(o_s, w["wuv"]).astype(bf16)
        g_t = g.reshape(nS, Ls, cdim).transpose(1, 0, 2)
        cb_t, nbuf_t = _conv_sample(g_t, state_conv[l].transpose(1, 0, 2), w)
        cb = cb_t.transpose(1, 0, 2).reshape(nS * Ls, cdim)
        gates3 = _pad_tokens(gates.reshape(nS, Ls, -1), C)
        hg, c1, n1, m1 = _mlstm_sample(
            _pad_tokens(mqkv.reshape(nS, Ls, -1), C), _pad_tokens(so.reshape(nS, Ls, -1), C), gates3,
            gates3.transpose(0, 2, 1), state_C, state_n[l],
            jnp.broadcast_to(state_m[l][:, :, None], (nS, mheads, LANES)), l, w["ml_norm_g"], dims=dims, n_new=Ls)
        hc = hg[:, :Ls].reshape(nS * Ls, mdim)
        x1, qx = _merge(xs, va, cb, hc, w, alpha=alpha)
        ctx = _xattn_sample(_pad_tokens(qx.reshape(nS, Ls, D), C), mem_k, mem_v, l, dims=dims)
        xs = _post(ctx[:, :Ls].reshape(nS * Ls, D), x1, w, alpha=alpha)
        for lst, val in zip(outs_s, (ckv.reshape(nS, Ls, kvl), kpe.reshape(nS, Ls, rope),
                                     nbuf_t.transpose(1, 0, 2), c1, n1, m1[:, :, 0])):
            lst.append(val)
    return (xp.reshape(nP, Lp, D), xs.reshape(nS, Ls, D), *[jnp.stack(v) for v in outs_p],
            *[jnp.stack(v) for v in outs_s])
```

```python
import functools
import math

import jax
import jax.numpy as jnp
from jax import lax
from jax.experimental import pallas as pl
from jax.experimental.pallas import tpu as pltpu

f32 = jnp.float32
bf16 = jnp.bfloat16

LN_EPS = 1e-5
RMS_EPS = 1e-6
ROPE_BASE = 10000.0
LANES = 128
VMEM_LIMIT = 56 * 1024 * 1024
ML_CHUNK_PROMPT = 256
SAMPLE_PAD = 16
SAMPLE_PAGES_PER_STEP = 32
NEG_INF = float("-inf")
LOG2E = math.log2(math.e)
ATTN_KEY_BLOCK = 256
ATTN_QUERY_TILE = 512
ATTN_HEADS_PER_GROUP = 2


def _cparams(*sem):
    return pltpu.CompilerParams(dimension_semantics=sem, vmem_limit_bytes=VMEM_LIMIT)


def _tile(n, pref):
    t = min(n, pref)
    while n % t:
        t //= 2
    return t


def _dot(a, b):
    return jnp.dot(a.astype(bf16), b.astype(bf16), preferred_element_type=f32)


def _dot_nt(a, b):
    return lax.dot_general(a.astype(bf16), b.astype(bf16), (((1,), (1,)), ((), ())),
                           preferred_element_type=f32)


def _sigmoid(x):
    return 1.0 / (1.0 + jnp.exp(-x))


def _log_sigmoid(x):
    return jnp.minimum(x, 0.0) - jnp.log(1.0 + jnp.exp(-jnp.abs(x)))


def _layer_norm(x, g, b):
    mu = jnp.mean(x, -1, keepdims=True)
    xc = x - mu
    var = jnp.mean(xc * xc, -1, keepdims=True)
    return xc * lax.rsqrt(var + LN_EPS) * g + b


def _rms_norm(x, g):
    return x * lax.rsqrt(jnp.mean(x * x, -1, keepdims=True) + RMS_EPS) * g


def _full(shape):
    nd = len(shape)
    return pl.BlockSpec(shape, lambda *_: (0,) * nd)


def _inproj_kernel(x_ref, cos_ref, sin_ref, wa_ref, ba_ref, qg_ref, wuq_ref, kg_ref, wuk_ref,
                   wc_ref, bc_ref, wm_ref, bm_ref, wi_ref, bi_ref,
                   q_ref, kv_ref, ckv_ref, kpe_ref, g_ref, mqkv_ref, so_ref, gates_ref,
                   *, ql, kvl, heads, nope, half, cdim, mdim, k_scale):
    x = x_ref[...].astype(bf16)
    cos = cos_ref[...]
    sin = sin_ref[...]
    hp = heads * half

    za = jnp.dot(x, wa_ref[...], preferred_element_type=f32) + ba_ref[...]
    cq = _rms_norm(za[:, :ql], qg_ref[...])
    ckv = _rms_norm(za[:, ql:ql + kvl], kg_ref[...])
    k1 = za[:, ql + kvl:ql + kvl + hp]
    k2 = za[:, ql + kvl + hp:]
    k1r = k1 * cos - k2 * sin
    k2r = k2 * cos + k1 * sin
    ckv_ref[...] = ckv
    kpe_ref[...] = jnp.concatenate([k1r[:, :half], k2r[:, :half]], axis=-1)
    kv_ref[:, :kvl] = ckv.astype(bf16)
    kv_ref[:, kvl:kvl + hp] = k1r.astype(bf16)
    kv_ref[:, kvl + hp:] = k2r.astype(bf16)

    qq = _dot(cq, wuq_ref[...])
    hn = heads * nope
    q1 = qq[:, hn:hn + hp]
    q2 = qq[:, hn + hp:]
    q1r = q1 * cos - q2 * sin
    q2r = q2 * cos + q1 * sin
    lane_head = lax.broadcasted_iota(jnp.int32, (1, hp), 1) // half
    for h in range(heads):
        q_lat = _dot(qq[:, h * nope:(h + 1) * nope], wuk_ref[h])
        q_ref[h, :, :kvl] = q_lat.astype(bf16)
        mine = lane_head == h
        q_ref[h, :, kvl:kvl + hp] = jnp.where(mine, q1r, 0.0).astype(bf16)
        q_ref[h, :, kvl + hp:] = jnp.where(mine, q2r, 0.0).astype(bf16)

    u = jnp.dot(x, wc_ref[...], preferred_element_type=f32) + bc_ref[...]
    g_ref[...] = u[:, :cdim] * _sigmoid(u[:, cdim:])

    zm = jnp.dot(x, wm_ref[...], preferred_element_type=f32) + bm_ref[...]
    mqkv_ref[:, :mdim] = zm[:, :mdim].astype(bf16)
    mqkv_ref[:, mdim:2 * mdim] = (zm[:, mdim:2 * mdim] * k_scale).astype(bf16)
    mqkv_ref[:, 2 * mdim:] = zm[:, 2 * mdim:3 * mdim].astype(bf16)
    so_ref[...] = _sigmoid(zm[:, 3 * mdim:])
    zi = jnp.dot(x, wi_ref[...], preferred_element_type=f32) + bi_ref[...]
    gates_ref[...] = zi[:, :gates_ref.shape[1]]


def _inproj(x, cos_t, sin_t, w, *, dims):
    T, D = x.shape
    tm = _tile(T, 256)
    tab_blocks = cos_t.shape[0] // tm
    H, kvl, ql = dims["heads"], dims["kvl"], dims["ql"]
    hp = H * dims["half"]
    qk = kvl + 2 * hp
    cdim, mdim, ng = dims["cdim"], dims["mdim"], dims["ngates"]
    row = lambda i: (i, 0)
    tab = lambda i: (i % tab_blocks, 0)
    kern = functools.partial(_inproj_kernel, ql=ql, kvl=kvl, heads=H, nope=dims["nope"], half=dims["half"],
                             cdim=cdim, mdim=mdim, k_scale=dims["mhd"] ** -0.5)
    weights = (w["wa"], w["ba"], w["qg"], w["wuq"], w["kg"], w["wuk"], w["wc"], w["bc"], w["wm"], w["bm"],
               w["wi"], w["bi"])
    return pl.pallas_call(
        kern,
        grid=(T // tm,),
        in_specs=[pl.BlockSpec((tm, D), row), pl.BlockSpec((tm, hp), tab), pl.BlockSpec((tm, hp), tab)]
        + [_full(a.shape) for a in weights],
        out_specs=[pl.BlockSpec((H, tm, qk), lambda i: (0, i, 0)), pl.BlockSpec((tm, qk), row),
                   pl.BlockSpec((tm, kvl), row), pl.BlockSpec((tm, 2 * dims["half"]), row),
                   pl.BlockSpec((tm, cdim), row), pl.BlockSpec((tm, 3 * mdim), row),
                   pl.BlockSpec((tm, mdim), row), pl.BlockSpec((tm, ng), row)],
        out_shape=[jax.ShapeDtypeStruct((H, T, qk), bf16), jax.ShapeDtypeStruct((T, qk), bf16),
                   jax.ShapeDtypeStruct((T, kvl), f32), jax.ShapeDtypeStruct((T, 2 * dims["half"]), f32),
                   jax.ShapeDtypeStruct((T, cdim), f32), jax.ShapeDtypeStruct((T, 3 * mdim), bf16),
                   jax.ShapeDtypeStruct((T, mdim), f32), jax.ShapeDtypeStruct((T, ng), f32)],
        compiler_params=_cparams("parallel"),
        name="inproj",
    )(x, cos_t, sin_t, *weights)


def _attn_prompt_kernel(q_ref, kv_ref, o_ref, m_ref, l_ref, acc_ref, *, tk, kvl, c_exp, hg):
    i = pl.program_id(1)
    heads, tq, qk = q_ref.shape
    nsub = tq // tk
    nchunk = tk // LANES
    nhg = heads // hg
    rows = hg * tk
    groups = [(h, t) for t in range(nsub) for h in range(nhg)]
    m_ref[...] = jnp.full(m_ref.shape, NEG_INF, f32)
    l_ref[...] = jnp.zeros(l_ref.shape, f32)
    acc_ref[...] = jnp.zeros(acc_ref.shape, f32)

    def run(blocks):
        items = [(j, diag, g) for (j, diag) in blocks for g in groups if diag is None or g[1] >= diag]

        def keys(j):
            return kv_ref[pl.ds(pl.multiple_of(j * tk, tk), tk), :]

        def scores(item):
            j, _, g = item
            q = q_ref[g[0] * hg:(g[0] + 1) * hg, g[1] * tk:(g[1] + 1) * tk, :]
            return _dot_nt(q.reshape(rows, qk), keys(j))

        s_next = scores(items[0])
        for idx, (j, diag, (h, t)) in enumerate(items):
            gi = t * nhg + h
            s = s_next
            if idx + 1 < len(items):
                s_next = scores(items[idx + 1])
            v = keys(j)[:, :kvl]
            if diag is not None and t == diag:
                qpos = lax.broadcasted_iota(jnp.int32, (rows, tk), 0) % tk
                kpos = lax.broadcasted_iota(jnp.int32, (rows, tk), 1)
                s = jnp.where(kpos <= qpos, s, NEG_INF)
            chunks = [s[:, c * LANES:(c + 1) * LANES] for c in range(nchunk)]
            m_prev = m_ref[gi]
            m_cur = jnp.max(functools.reduce(jnp.maximum, chunks), -1, keepdims=True)
            m_new = jnp.maximum(m_prev, m_cur)
            alpha = jnp.exp2((m_prev - m_new) * c_exp)
            ps = [jnp.exp2((ch - m_new) * c_exp) for ch in chunks]
            l_ref[gi] = alpha * l_ref[gi] + functools.reduce(jnp.add, ps)
            p = jnp.concatenate([x.astype(bf16) for x in ps], axis=-1)
            acc_ref[gi] = (jnp.concatenate([alpha] * (kvl // LANES), axis=-1) * acc_ref[gi]
                           + jnp.dot(p, v, preferred_element_type=f32))
            m_ref[gi] = m_new

    def body(jj, carry):
        run([(jj * nsub + u, None) for u in range(nsub)])
        return carry

    lax.fori_loop(0, i, body, 0)
    run([(i * nsub + d, d) for d in range(nsub)])
    for (h, t) in groups:
        gi = t * nhg + h
        o = acc_ref[gi] / jnp.sum(l_ref[gi], -1, keepdims=True)
        o_ref[h * hg:(h + 1) * hg, t * tk:(t + 1) * tk, :] = o.reshape(hg, tk, kvl).astype(o_ref.dtype)


def _attn_prompt(q, kv, n_seq, *, dims):
    H, T, qk = q.shape
    L = T // n_seq
    kvl = dims["kvl"]
    tk = _tile(L, ATTN_KEY_BLOCK)
    tq = _tile(L, ATTN_QUERY_TILE)
    nq = L // tq
    hg = math.gcd(H, ATTN_HEADS_PER_GROUP)
    ngroups = (H // hg) * (tq // tk)
    kern = functools.partial(_attn_prompt_kernel, tk=tk, kvl=kvl, c_exp=dims["attn_scale"] * LOG2E, hg=hg)
    return pl.pallas_call(
        kern,
        grid=(n_seq, nq),
        in_specs=[pl.BlockSpec((H, tq, qk), lambda n, i: (0, n * nq + i, 0)),
                  pl.BlockSpec((L, qk), lambda n, i: (n, 0))],
        out_specs=pl.BlockSpec((H, tq, kvl), lambda n, i: (0, n * nq + i, 0)),
        out_shape=jax.ShapeDtypeStruct((H, T, kvl), bf16),
        scratch_shapes=[pltpu.VMEM((ngroups, hg * tk, LANES), f32), pltpu.VMEM((ngroups, hg * tk, LANES), f32),
                        pltpu.VMEM((ngroups, hg * tk, kvl), f32)],
        compiler_params=_cparams("parallel", "parallel"),
        name="attn_prompt",
    )(q, kv)


def _attn_sample_kernel(pt_ref, q_ref, qpe_ref, kvn_ref, *rest, pages, kvl, c_exp, n_new):
    ckv_refs = rest[:pages]
    kpe_refs = rest[pages:2 * pages]
    o_ref = rest[2 * pages]
    kbuf, m_ref, l_ref, acc_ref = rest[2 * pages + 1:]
    j = pl.program_id(1)
    psz = ckv_refs[0].shape[0]

    @pl.when(j == 0)
    def _():
        m_ref[...] = jnp.full(m_ref.shape, NEG_INF, f32)
        l_ref[...] = jnp.zeros(l_ref.shape, f32)
        acc_ref[...] = jnp.zeros(acc_ref.shape, f32)

    def online(s, v):
        m_prev = m_ref[...]
        m_new = jnp.maximum(m_prev, jnp.max(s, -1, keepdims=True))
        alpha = jnp.exp2((m_prev - m_new) * c_exp)
        p = jnp.exp2((s - m_new) * c_exp)
        l_ref[...] = alpha * l_ref[...] + jnp.sum(p, -1, keepdims=True)
        acc_ref[...] = alpha * acc_ref[...] + _dot(p, v)
        m_ref[...] = m_new

    for p in range(pages):
        kbuf[p * psz:(p + 1) * psz, :] = ckv_refs[p][...].astype(bf16)
    kpe_t = jnp.concatenate([kpe_refs[p][...].astype(bf16) for p in range(pages)], axis=-1)
    k = kbuf[...]
    online(_dot_nt(q_ref[:, :kvl], k) + jnp.dot(qpe_ref[...], kpe_t, preferred_element_type=f32), k)

    @pl.when(j == pl.num_programs(1) - 1)
    def _():
        kn = kvn_ref[...]
        s = _dot_nt(q_ref[...], kn)
        tok = lax.broadcasted_iota(jnp.int32, s.shape, 0) % n_new
        col = lax.broadcasted_iota(jnp.int32, s.shape, 1)
        s = jnp.where(col <= tok, s, NEG_INF)
        online(s, kn[:, :kvl])
        o_ref[...] = acc_ref[...] / l_ref[...]


def _attn_sample(page_table, q_s, qpe_s, kv_new, cache_ckv, cache_kpe_t, layer, *, dims, n_new):
    nb, rows, qk = q_s.shape
    n_pages = page_table.shape[1]
    psz = cache_ckv.shape[2]
    kvl, rope = dims["kvl"], 2 * dims["half"]
    pages = _tile(n_pages, SAMPLE_PAGES_PER_STEP)
    steps = n_pages // pages

    def page_map(p):
        return lambda b, j, pt: (layer, pt[b, j * pages + p], 0, 0)

    kern = functools.partial(_attn_sample_kernel, pages=pages, kvl=kvl, c_exp=dims["attn_scale"] * LOG2E,
                             n_new=n_new)
    grid_spec = pltpu.PrefetchScalarGridSpec(
        num_scalar_prefetch=1,
        grid=(nb, steps),
        in_specs=[pl.BlockSpec((None, rows, qk), lambda b, j, pt: (b, 0, 0)),
                  pl.BlockSpec((None, rows, rope), lambda b, j, pt: (b, 0, 0)),
                  pl.BlockSpec((None, LANES, qk), lambda b, j, pt: (b, 0, 0))]
        + [pl.BlockSpec((None, None, psz, kvl), page_map(p)) for p in range(pages)]
        + [pl.BlockSpec((None, None, rope, psz), page_map(p)) for p in range(pages)],
        out_specs=pl.BlockSpec((None, rows, kvl), lambda b, j, pt: (b, 0, 0)),
        scratch_shapes=[pltpu.VMEM((pages * psz, kvl), bf16), pltpu.VMEM((rows, 1), f32),
                        pltpu.VMEM((rows, 1), f32), pltpu.VMEM((rows, kvl), f32)],
    )
    return pl.pallas_call(
        kern,
        grid_spec=grid_spec,
        out_shape=jax.ShapeDtypeStruct((nb, rows, kvl), f32),
        compiler_params=_cparams("parallel", "arbitrary"),
        name="attn_sample",
    )(page_table, q_s, qpe_s, kv_new, *([cache_ckv] * pages), *([cache_kpe_t] * pages))


def _attn_out_kernel(o_ref, wuv_ref, va_ref):
    heads = o_ref.shape[0]
    vd = wuv_ref.shape[2]
    for h in range(heads):
        va_ref[:, h * vd:(h + 1) * vd] = jnp.dot(o_ref[h], wuv_ref[h], preferred_element_type=f32)


def _attn_out(o, wuv):
    H, T, kvl = o.shape
    vd = wuv.shape[2]
    tm = _tile(T, 512)
    return pl.pallas_call(
        _attn_out_kernel,
        grid=(T // tm,),
        in_specs=[pl.BlockSpec((H, tm, kvl), lambda i: (0, i, 0)), _full(wuv.shape)],
        out_specs=pl.BlockSpec((tm, H * vd), lambda i: (i, 0)),
        out_shape=jax.ShapeDtypeStruct((T, H * vd), f32),
        compiler_params=_cparams("parallel"),
        name="attn_out",
    )(o, wuv)


def _conv_prompt_kernel(g_ref, cw_ref, cb_ref, lg_ref, lb_ref, act_ref, buf_ref, xp_ref, sh_ref, *, taps, rc):
    i = pl.program_id(1)
    tm, cdim = g_ref.shape
    head = xp_ref.shape[0] - tm
    keep = taps - 1

    @pl.when(i == 0)
    def _():
        xp_ref[:head, :] = jnp.zeros((head, cdim), f32)

    xp_ref[head:, :] = g_ref[...]
    w = cw_ref[...]
    lead = head - keep
    span = sh_ref.shape[1]
    for r in range(1, 8):
        sh_ref[r - 1] = xp_ref[r:r + span, :]
    for c in range(tm // rc):
        acc = jnp.zeros((rc, cdim), f32)
        for t in range(taps):
            a, r = divmod(lead + t, 8)
            rows = slice(c * rc + 8 * a, c * rc + 8 * a + rc)
            acc = acc + (xp_ref[rows, :] if r == 0 else sh_ref[r - 1, rows, :]) * w[t:t + 1, :]
        y = _layer_norm(acc + cb_ref[...], lg_ref[...], lb_ref[...])
        act_ref[c * rc:(c + 1) * rc, :] = (y * _sigmoid(y)).astype(act_ref.dtype)

    @pl.when(i == pl.num_programs(1) - 1)
    def _():
        buf_ref[...] = xp_ref[head + tm - keep:, :]

    xp_ref[:head, :] = xp_ref[tm:, :]


def _conv_prompt(g, n_seq, w):
    T, cdim = g.shape
    L = T // n_seq
    taps = w["conv_w"].shape[0]
    tm = _tile(L, 256)
    nt = L // tm
    head = 32
    assert taps - 1 <= head <= tm
    kern = functools.partial(_conv_prompt_kernel, taps=taps, rc=_tile(tm, 32))
    return pl.pallas_call(
        kern,
        grid=(n_seq, nt),
        in_specs=[pl.BlockSpec((tm, cdim), lambda n, i: (n * nt + i, 0)), _full(w["conv_w"].shape),
                  _full(w["conv_b"].shape), _full(w["conv_ln_g"].shape), _full(w["conv_ln_b"].shape)],
        out_specs=[pl.BlockSpec((tm, cdim), lambda n, i: (n * nt + i, 0)),
                   pl.BlockSpec((None, taps - 1, cdim), lambda n, i: (n, 0, 0))],
        out_shape=[jax.ShapeDtypeStruct((T, cdim), bf16), jax.ShapeDtypeStruct((n_seq, taps - 1, cdim), f32)],
        scratch_shapes=[pltpu.VMEM((head + tm, cdim), f32), pltpu.VMEM((7, head + tm - 8, cdim), f32)],
        compiler_params=_cparams("parallel", "arbitrary"),
        name="conv_prompt",
    )(g, w["conv_w"], w["conv_b"], w["conv_ln_g"], w["conv_ln_b"])


def _conv_sample_kernel(g_ref, buf_ref, cw_ref, cb_ref, lg_ref, lb_ref, act_ref, nbuf_ref, *, taps):
    n_new = g_ref.shape[0]
    keep = taps - 1
    w = cw_ref[...]

    def xp(r):
        return buf_ref[r] if r < keep else g_ref[r - keep]

    for t in range(n_new):
        acc = xp(t) * w[0:1, :]
        for k in range(1, taps):
            acc = acc + xp(t + k) * w[k:k + 1, :]
        y = _layer_norm(acc + cb_ref[...], lg_ref[...], lb_ref[...])
        act_ref[t] = (y * _sigmoid(y)).astype(act_ref.dtype)
    for r in range(keep):
        nbuf_ref[r] = xp(r + n_new)


def _conv_sample(g_t, buf_t, w):
    n_new, nb, cdim = g_t.shape
    taps = w["conv_w"].shape[0]
    sb = _tile(nb, 32)
    kern = functools.partial(_conv_sample_kernel, taps=taps)
    return pl.pallas_call(
        kern,
        grid=(nb // sb,),
        in_specs=[pl.BlockSpec((n_new, sb, cdim), lambda i: (0, i, 0)),
                  pl.BlockSpec((taps - 1, sb, cdim), lambda i: (0, i, 0)), _full(w["conv_w"].shape),
                  _full(w["conv_b"].shape), _full(w["conv_ln_g"].shape), _full(w["conv_ln_b"].shape)],
        out_specs=[pl.BlockSpec((n_new, sb, cdim), lambda i: (0, i, 0)),
                   pl.BlockSpec((taps - 1, sb, cdim), lambda i: (0, i, 0))],
        out_shape=[jax.ShapeDtypeStruct((n_new, nb, cdim), bf16), jax.ShapeDtypeStruct((taps - 1, nb, cdim), f32)],
        compiler_params=_cparams("parallel"),
        name="conv_sample",
    )(g_t, buf_t, w["conv_w"], w["conv_b"], w["conv_ln_g"], w["conv_ln_b"])


def _head_norm_gate(h, g_row, so):
    mu = jnp.mean(h, -1, keepdims=True)
    hc = h - mu
    var = jnp.mean(hc * hc, -1, keepdims=True)
    return hc * lax.rsqrt(var + LN_EPS) * g_row * so


def _mlstm_chunk(q, k, v, ig_col, ig_row, lf_col, lf_row, m0, c0, n0, last):
    c = q.shape[0]
    row = lax.broadcasted_iota(jnp.int32, (c, c), 0)
    col = lax.broadcasted_iota(jnp.int32, (c, c), 1)
    causal = col <= row
    b_col = jnp.sum(jnp.where(causal, lf_row, 0.0), -1, keepdims=True)
    b_row = jnp.sum(jnp.where(row <= col, lf_col, 0.0), 0, keepdims=True)
    d = jnp.where(causal, b_col - b_row + ig_row, NEG_INF)
    inter = b_col + m0
    m = jnp.maximum(inter, jnp.max(d, -1, keepdims=True))
    dw = jnp.exp(d - m)
    iw = jnp.exp(inter - m)
    sw = _dot_nt(q, k) * dw
    num = iw * _dot(q, c0) + _dot(sw, v)
    den = iw * jnp.sum(q.astype(f32) * n0, -1, keepdims=True) + jnp.sum(sw, -1, keepdims=True)
    hh = num / jnp.maximum(jnp.abs(den), jnp.exp(-m))
    m_last = m[last:last + 1, :]
    b_last = b_col[last:last + 1, :]
    w_last = jnp.exp(b_last - b_col + ig_col - m_last)
    decay = jnp.exp(b_last + m0 - m_last)
    kw = w_last * k.astype(f32)
    c1 = decay * c0 + _dot(kw.T, v)
    n1 = decay * n0 + jnp.sum(kw, 0, keepdims=True)
    return hh, c1, n1, m_last


def _mlstm_prompt_kernel(qkv_ref, so_ref, gc_ref, gr_ref, ng_ref, hg_ref, c_out, n_out, m_out,
                         c_s, n_s, m_s, *, heads, hd):
    i = pl.program_id(1)
    c = qkv_ref.shape[0]
    mdim = heads * hd

    @pl.when(i == 0)
    def _():
        c_s[...] = jnp.zeros(c_s.shape, f32)
        n_s[...] = jnp.zeros(n_s.shape, f32)
        m_s[...] = jnp.zeros(m_s.shape, f32)

    gcol = gc_ref[...]
    grow = gr_ref[...]
    for h in range(heads):
        hh, c1, n1, m1 = _mlstm_chunk(
            qkv_ref[:, h * hd:(h + 1) * hd], qkv_ref[:, mdim + h * hd:mdim + (h + 1) * hd],
            qkv_ref[:, 2 * mdim + h * hd:2 * mdim + (h + 1) * hd],
            gcol[:, h:h + 1], grow[h:h + 1, :], _log_sigmoid(gcol[:, heads + h:heads + h + 1]),
            _log_sigmoid(grow[heads + h:heads + h + 1, :]), m_s[h][0:1, 0:1], c_s[h], n_s[h][0:1, :], c - 1)
        c_s[h] = c1
        n_s[h] = jnp.broadcast_to(n1, n_s.shape[1:])
        m_s[h] = jnp.broadcast_to(m1, m_s.shape[1:])
        hg_ref[:, h * hd:(h + 1) * hd] = _head_norm_gate(
            hh, ng_ref[:, h * hd:(h + 1) * hd], so_ref[:, h * hd:(h + 1) * hd]).astype(hg_ref.dtype)

    @pl.when(i == pl.num_programs(1) - 1)
    def _():
        c_out[...] = c_s[...]
        n_out[...] = n_s[...]
        m_out[...] = m_s[...]


def _mlstm_prompt(mqkv, so, gates, n_seq, ng, *, dims):
    T, _ = mqkv.shape
    L = T // n_seq
    heads, hd = dims["mheads"], dims["mhd"]
    mdim = heads * hd
    c = _tile(L, ML_CHUNK_PROMPT)
    nc = L // c
    ngates = gates.shape[1]
    gates_row = gates.reshape(n_seq * nc, c, ngates).transpose(0, 2, 1)
    kern = functools.partial(_mlstm_prompt_kernel, heads=heads, hd=hd)
    rowmap = lambda n, i: (n * nc + i, 0)
    return pl.pallas_call(
        kern,
        grid=(n_seq, nc),
        in_specs=[pl.BlockSpec((c, 3 * mdim), rowmap), pl.BlockSpec((c, mdim), rowmap),
                  pl.BlockSpec((c, ngates), rowmap),
                  pl.BlockSpec((None, ngates, c), lambda n, i: (n * nc + i, 0, 0)), _full(ng.shape)],
        out_specs=[pl.BlockSpec((c, mdim), rowmap),
                   pl.BlockSpec((None, heads, hd, hd), lambda n, i: (n, 0, 0, 0)),
                   pl.BlockSpec((None, heads, 8, hd), lambda n, i: (n, 0, 0, 0)),
                   pl.BlockSpec((None, heads, 8, LANES), lambda n, i: (n, 0, 0, 0))],
        out_shape=[jax.ShapeDtypeStruct((T, mdim), bf16), jax.ShapeDtypeStruct((n_seq, heads, hd, hd), f32),
                   jax.ShapeDtypeStruct((n_seq, heads, 8, hd), f32),
                   jax.ShapeDtypeStruct((n_seq, heads, 8, LANES), f32)],
        scratch_shapes=[pltpu.VMEM((heads, hd, hd), f32), pltpu.VMEM((heads, 8, hd), f32),
                        pltpu.VMEM((heads, 8, LANES), f32)],
        compiler_params=_cparams("parallel", "arbitrary"),
        name="mlstm_prompt",
    )(mqkv, so, gates, gates_row, ng)


def _mlstm_sample_kernel(qkv_ref, so_ref, gc_ref, gr_ref, c0_ref, n0_ref, m0_ref, ng_ref,
                         hg_ref, c_out, n_out, m_out, *, heads, hd, n_new):
    sb, c, _ = qkv_ref.shape
    mdim = heads * hd
    valid_col = lax.broadcasted_iota(jnp.int32, (c, 1), 0) < n_new
    valid_row = lax.broadcasted_iota(jnp.int32, (1, c), 1) < n_new

    def seq(b, carry):
        gcol = gc_ref[b]
        grow = gr_ref[b]
        for h in range(heads):
            hh, c1, n1, m1 = _mlstm_chunk(
                qkv_ref[b, :, h * hd:(h + 1) * hd], qkv_ref[b, :, mdim + h * hd:mdim + (h + 1) * hd],
                qkv_ref[b, :, 2 * mdim + h * hd:2 * mdim + (h + 1) * hd],
                jnp.where(valid_col, gcol[:, h:h + 1], NEG_INF), jnp.where(valid_row, grow[h:h + 1, :], NEG_INF),
                jnp.where(valid_col, _log_sigmoid(gcol[:, heads + h:heads + h + 1]), 0.0),
                jnp.where(valid_row, _log_sigmoid(grow[heads + h:heads + h + 1, :]), 0.0),
                m0_ref[b, h:h + 1, 0:1], c0_ref[b, h], n0_ref[b, h:h + 1, :], n_new - 1)
            c_out[b, h] = c1
            n_out[b, h:h + 1, :] = n1
            m_out[b, h:h + 1, :] = jnp.broadcast_to(m1, (1, m_out.shape[2]))
            hg_ref[b, :, h * hd:(h + 1) * hd] = _head_norm_gate(
                hh, ng_ref[:, h * hd:(h + 1) * hd], so_ref[b, :, h * hd:(h + 1) * hd]).astype(hg_ref.dtype)
        return carry

    lax.fori_loop(0, sb, seq, 0)


def _mlstm_sample(mqkv_p, so_p, gcol_p, grow_p, state_c, n0, m0b, layer, ng, *, dims, n_new):
    nb, c, _ = mqkv_p.shape
    heads, hd = dims["mheads"], dims["mhd"]
    mdim = heads * hd
    ngates = gcol_p.shape[2]
    sb = _tile(nb, 8)
    kern = functools.partial(_mlstm_sample_kernel, heads=heads, hd=hd, n_new=n_new)
    seqmap = lambda i: (i, 0, 0)
    return pl.pallas_call(
        kern,
        grid=(nb // sb,),
        in_specs=[pl.BlockSpec((sb, c, 3 * mdim), seqmap), pl.BlockSpec((sb, c, mdim), seqmap),
                  pl.BlockSpec((sb, c, ngates), seqmap), pl.BlockSpec((sb, ngates, c), seqmap),
                  pl.BlockSpec((None, sb, heads, hd, hd), lambda i: (layer, i, 0, 0, 0)),
                  pl.BlockSpec((sb, heads, hd), seqmap), pl.BlockSpec((sb, heads, LANES), seqmap),
                  _full(ng.shape)],
        out_specs=[pl.BlockSpec((sb, c, mdim), seqmap), pl.BlockSpec((sb, heads, hd, hd), lambda i: (i, 0, 0, 0)),
                   pl.BlockSpec((sb, heads, hd), seqmap), pl.BlockSpec((sb, heads, LANES), seqmap)],
        out_shape=[jax.ShapeDtypeStruct((nb, c, mdim), bf16), jax.ShapeDtypeStruct((nb, heads, hd, hd), f32),
                   jax.ShapeDtypeStruct((nb, heads, hd), f32), jax.ShapeDtypeStruct((nb, heads, LANES), f32)],
        compiler_params=_cparams("parallel"),
        name="mlstm_sample",
    )(mqkv_p, so_p, gcol_p, grow_p, state_c, n0, m0b, ng)


def _merge_kernel(x_ref, va_ref, cb_ref, hc_ref, wg_ref, bg_ref, woa_ref, wob_ref, woc_ref, wout_ref,
                  lg_ref, lb_ref, wq_ref, x1_ref, qx_ref, *, alpha):
    x = x_ref[...]
    xb = x.astype(bf16)
    d = x.shape[1]
    merged = jnp.zeros(x.shape, f32)
    for k, (b_ref, w_ref) in enumerate(((va_ref, woa_ref), (cb_ref, wob_ref), (hc_ref, woc_ref))):
        gate = _sigmoid(jnp.dot(xb, wg_ref[:, k * d:(k + 1) * d], preferred_element_type=f32)
                        + bg_ref[:, k * d:(k + 1) * d])
        merged = merged + gate * _dot(b_ref[...], w_ref[...])
    mix = _dot(merged, wout_ref[...])
    x1 = _layer_norm(alpha * x + mix, lg_ref[...], lb_ref[...])
    x1_ref[...] = x1
    qx_ref[...] = _dot(x1, wq_ref[...]).astype(qx_ref.dtype)


def _merge(x, va, cb, hc, w, *, alpha):
    T, D = x.shape
    tm = _tile(T, 256)
    row = lambda i: (i, 0)
    weights = (w["wg"], w["bg"], w["w_oa"], w["w_ob"], w["w_oc"], w["w_out"], w["ln1_g"], w["ln1_b"], w["xa_wq"])
    return pl.pallas_call(
        functools.partial(_merge_kernel, alpha=alpha),
        grid=(T // tm,),
        in_specs=[pl.BlockSpec((tm, D), row), pl.BlockSpec((tm, va.shape[1]), row),
                  pl.BlockSpec((tm, cb.shape[1]), row), pl.BlockSpec((tm, hc.shape[1]), row)]
        + [_full(a.shape) for a in weights],
        out_specs=[pl.BlockSpec((tm, D), row), pl.BlockSpec((tm, D), row)],
        out_shape=[jax.ShapeDtypeStruct((T, D), f32), jax.ShapeDtypeStruct((T, D), bf16)],
        compiler_params=_cparams("parallel"),
        name="merge",
    )(x, va, cb, hc, *weights)


def _mem_kv_kernel(mem_ref, wk_ref, wv_ref, mk_ref, mv_ref):
    mem = mem_ref[...].astype(bf16)
    mk_ref[...] = jnp.dot(mem, wk_ref[...], preferred_element_type=f32)
    mv_ref[...] = jnp.dot(mem, wv_ref[...], preferred_element_type=f32)


def _mem_kv(mem, wk, wv):
    R, D = mem.shape
    tm = _tile(R, 256)
    row = lambda i: (i, 0)
    return pl.pallas_call(
        _mem_kv_kernel,
        grid=(R // tm,),
        in_specs=[pl.BlockSpec((tm, D), row), _full(wk.shape), _full(wv.shape)],
        out_specs=[pl.BlockSpec((tm, D), row), pl.BlockSpec((tm, D), row)],
        out_shape=[jax.ShapeDtypeStruct((R, D), f32), jax.ShapeDtypeStruct((R, D), f32)],
        compiler_params=_cparams("parallel"),
        name="mem_kv",
    )(mem, wk, wv)


def _xattn_heads(q, mk, mv, heads, hd, scale):
    outs = []
    for h in range(heads):
        s = _dot_nt(q[:, h * hd:(h + 1) * hd], mk[:, h * hd:(h + 1) * hd]) * scale
        e = jnp.exp(s - jnp.max(s, -1, keepdims=True))
        p = e / jnp.sum(e, -1, keepdims=True)
        outs.append(_dot(p, mv[:, h * hd:(h + 1) * hd]))
    return outs


def _xattn_prompt_kernel(q_ref, mk_ref, mv_ref, ctx_ref, *, heads, hd, scale):
    outs = _xattn_heads(q_ref[...], mk_ref[...], mv_ref[...], heads, hd, scale)
    for h in range(heads):
        ctx_ref[:, h * hd:(h + 1) * hd] = outs[h].astype(ctx_ref.dtype)


def _xattn_prompt(qx, mk, mv, n_seq, *, dims):
    T, D = qx.shape
    L = T // n_seq
    M = mk.shape[0] // n_seq
    tm = _tile(L, 512)
    nt = L // tm
    heads, hd = dims["xheads"], dims["xhd"]
    kern = functools.partial(_xattn_prompt_kernel, heads=heads, hd=hd, scale=hd ** -0.5)
    return pl.pallas_call(
        kern,
        grid=(n_seq, nt),
        in_specs=[pl.BlockSpec((tm, D), lambda n, i: (n * nt + i, 0)), pl.BlockSpec((M, D), lambda n, i: (n, 0)),
                  pl.BlockSpec((M, D), lambda n, i: (n, 0))],
        out_specs=pl.BlockSpec((tm, D), lambda n, i: (n * nt + i, 0)),
        out_shape=jax.ShapeDtypeStruct((T, D), bf16),
        compiler_params=_cparams("parallel", "parallel"),
        name="xattn_prompt",
    )(qx, mk, mv)


def _xattn_sample_kernel(q_ref, mk_ref, mv_ref, ctx_ref, *, heads, hd, scale):
    sb = q_ref.shape[0]

    def seq(b, carry):
        outs = _xattn_heads(q_ref[b], mk_ref[b], mv_ref[b], heads, hd, scale)
        for h in range(heads):
            ctx_ref[b, :, h * hd:(h + 1) * hd] = outs[h].astype(ctx_ref.dtype)
        return carry

    lax.fori_loop(0, sb, seq, 0)


def _xattn_sample(qx_p, mem_k, mem_v, layer, *, dims):
    nb, c, D = qx_p.shape
    M = mem_k.shape[2]
    sb = _tile(nb, 8)
    heads, hd = dims["xheads"], dims["xhd"]
    kern = functools.partial(_xattn_sample_kernel, heads=heads, hd=hd, scale=hd ** -0.5)
    memmap = lambda i: (layer, i, 0, 0)
    return pl.pallas_call(
        kern,
        grid=(nb // sb,),
        in_specs=[pl.BlockSpec((sb, c, D), lambda i: (i, 0, 0)), pl.BlockSpec((None, sb, M, D), memmap),
                  pl.BlockSpec((None, sb, M, D), memmap)],
        out_specs=pl.BlockSpec((sb, c, D), lambda i: (i, 0, 0)),
        out_shape=jax.ShapeDtypeStruct((nb, c, D), bf16),
        compiler_params=_cparams("parallel"),
        name="xattn_sample",
    )(qx_p, mem_k, mem_v)


def _post_kernel(ctx_ref, x1_ref, wo_ref, l2g_ref, l2b_ref, wup_ref, wdn_ref, l3g_ref, l3b_ref, x3_ref,
                 *, alpha, ff_chunk):
    x1 = x1_ref[...]
    x2 = _layer_norm(alpha * x1 + jnp.dot(ctx_ref[...], wo_ref[...], preferred_element_type=f32),
                     l2g_ref[...], l2b_ref[...])
    x2b = x2.astype(bf16)
    dff = wup_ref.shape[1]
    acc = jnp.zeros(x1.shape, f32)
    for c in range(dff // ff_chunk):
        hcol = jnp.maximum(jnp.dot(x2b, wup_ref[:, c * ff_chunk:(c + 1) * ff_chunk],
                                   preferred_element_type=f32), 0.0)
        acc = acc + _dot(hcol * hcol, wdn_ref[c * ff_chunk:(c + 1) * ff_chunk, :])
    x3_ref[...] = _layer_norm(alpha * x2 + acc, l3g_ref[...], l3b_ref[...])


def _post(ctx, x1, w, *, alpha):
    T, D = x1.shape
    tm = _tile(T, 256)
    row = lambda i: (i, 0)
    weights = (w["xa_wo"], w["ln2_g"], w["ln2_b"], w["w_up"], w["w_down"], w["ln3_g"], w["ln3_b"])
    return pl.pallas_call(
        functools.partial(_post_kernel, alpha=alpha, ff_chunk=_tile(w["w_up"].shape[1], 1024)),
        grid=(T // tm,),
        in_specs=[pl.BlockSpec((tm, D), row), pl.BlockSpec((tm, D), row)] + [_full(a.shape) for a in weights],
        out_specs=pl.BlockSpec((tm, D), row),
        out_shape=jax.ShapeDtypeStruct((T, D), f32),
        compiler_params=_cparams("parallel"),
        name="post",
    )(ctx, x1, *weights)


def _rope_tables(pos, half, heads):
    inv = ROPE_BASE ** (-jnp.arange(half, dtype=f32) / half)
    ang = pos.astype(f32)[:, None] * inv[None, :]
    return jnp.tile(jnp.cos(ang), (1, heads)), jnp.tile(jnp.sin(ang), (1, heads))


def _prep_layer(l, p, dims):
    ql, kvl, half, H, nope = dims["ql"], dims["kvl"], dims["half"], dims["heads"], dims["nope"]
    cdim, mdim, mheads = dims["cdim"], dims["mdim"], dims["mheads"]
    D = dims["d"]
    w_in, b_in = p["w_in"][l], p["b_in"][l]
    o = 0

    def take(n):
        nonlocal o
        cols = (w_in[:, o:o + n], b_in[o:o + n])
        o += n
        return cols

    (w_cq, b_cq), (w_kv, b_kv), (w_kp, b_kp) = take(ql), take(kvl), take(2 * half)
    (w_cu, b_cu) = take(2 * cdim)
    (w_m, b_m) = take(4 * mdim)
    (w_if, b_if) = take(2 * mheads)
    (w_g, b_g) = take(3 * D)
    tile_h = lambda a: jnp.tile(a, (1, H)) if a.ndim == 2 else jnp.tile(a, H)
    wa = jnp.concatenate([w_cq, w_kv, tile_h(w_kp[:, :half]), tile_h(w_kp[:, half:])], 1)
    ba = jnp.concatenate([b_cq, b_kv, tile_h(b_kp[:half]), tile_h(b_kp[half:])])
    w_uq = p["w_uq"][l]
    wuq = jnp.concatenate([w_uq[:, :, :nope].reshape(ql, H * nope),
                           w_uq[:, :, nope:nope + half].reshape(ql, H * half),
                           w_uq[:, :, nope + half:].reshape(ql, H * half)], 1)
    pad_if = LANES - 2 * mheads
    return {
        "wa": wa.astype(bf16), "ba": ba[None], "qg": p["q_norm_g"][l][None], "wuq": wuq.astype(bf16),
        "kg": p["kv_norm_g"][l][None], "wuk": p["w_uk"][l].transpose(1, 2, 0).astype(bf16),
        "wc": w_cu.astype(bf16), "bc": b_cu[None], "wm": w_m.astype(bf16), "bm": b_m[None],
        "wi": jnp.pad(w_if, ((0, 0), (0, pad_if))).astype(bf16), "bi": jnp.pad(b_if, (0, pad_if))[None],
        "wuv": p["w_uv"][l].transpose(1, 0, 2).astype(bf16),
        "conv_w": p["conv_w"][l], "conv_b": p["conv_b"][l][None], "conv_ln_g": p["conv_ln_g"][l][None],
        "conv_ln_b": p["conv_ln_b"][l][None],
        "ml_norm_g": p["ml_norm_g"][l].reshape(1, mdim),
        "wg": w_g.astype(bf16), "bg": b_g[None], "w_oa": p["w_oa"][l].astype(bf16),
        "w_ob": p["w_ob"][l].astype(bf16), "w_oc": p["w_oc"][l].astype(bf16), "w_out": p["w_out"][l].astype(bf16),
        "ln1_g": p["ln1_g"][l][None], "ln1_b": p["ln1_b"][l][None], "xa_wq": p["xa_wq"][l].astype(bf16),
        "xa_wk": p["xa_wk"][l].astype(bf16), "xa_wv": p["xa_wv"][l].astype(bf16),
        "xa_wo": p["xa_wo"][l].astype(bf16), "ln2_g": p["ln2_g"][l][None], "ln2_b": p["ln2_b"][l][None],
        "w_up": p["w_up"][l].astype(bf16), "w_down": p["w_down"][l].astype(bf16),
        "ln3_g": p["ln3_g"][l][None], "ln3_b": p["ln3_b"][l][None],
    }


def _pad_tokens(a, c):
    return jnp.pad(a, ((0, 0), (0, c - a.shape[1])) + ((0, 0),) * (a.ndim - 2))


def kernel(x_prompt, x_sample, mem_prompt, cache_ckv, cache_kpe, state_conv, state_C, state_n, state_m, cache_mem_k, cache_mem_v, page_table, w_in, b_in, q_norm_g, w_uq, kv_norm_g, w_uk, w_uv, w_oa, conv_w, conv_b, conv_ln_g, conv_ln_b, w_ob, ml_norm_g, w_oc, w_out, ln1_g, ln1_b, xa_wq, xa_wk, xa_wv, xa_wo, ln2_g, ln2_b, w_up, w_down, ln3_g, ln3_b):
    params = dict(w_in=w_in, b_in=b_in, q_norm_g=q_norm_g, w_uq=w_uq, kv_norm_g=kv_norm_g, w_uk=w_uk, w_uv=w_uv,
                  w_oa=w_oa, conv_w=conv_w, conv_b=conv_b, conv_ln_g=conv_ln_g, conv_ln_b=conv_ln_b, w_ob=w_ob,
                  ml_norm_g=ml_norm_g, w_oc=w_oc, w_out=w_out, ln1_g=ln1_g, ln1_b=ln1_b, xa_wq=xa_wq, xa_wk=xa_wk,
                  xa_wv=xa_wv, xa_wo=xa_wo, ln2_g=ln2_g, ln2_b=ln2_b, w_up=w_up, w_down=w_down, ln3_g=ln3_g,
                  ln3_b=ln3_b)
    depth, D, _ = w_in.shape
    nP, Lp, _ = x_prompt.shape
    nS, Ls, _ = x_sample.shape
    H, nope = w_uk.shape[2], w_uk.shape[3]
    rope = cache_kpe.shape[-1]
    half = rope // 2
    mheads, mhd = ml_norm_g.shape[1], ml_norm_g.shape[2]
    xheads, xhd = cache_mem_k.shape[-2], cache_mem_k.shape[-1]
    M = mem_prompt.shape[1]
    dims = dict(d=D, ql=q_norm_g.shape[1], kvl=kv_norm_g.shape[1], heads=H, nope=nope, half=half,
                cdim=conv_w.shape[2], mdim=mheads * mhd, mheads=mheads, mhd=mhd, ngates=2 * mheads,
                xheads=xheads, xhd=xhd, attn_scale=(nope + rope) ** -0.5)
    assert H * half == LANES and 2 * mheads <= LANES
    kvl, mdim, cdim, taps = dims["kvl"], dims["mdim"], dims["cdim"], conv_w.shape[1]
    alpha = (2 * depth) ** 0.25
    past_len = page_table.shape[1] * cache_ckv.shape[2]
    cos_p, sin_p = _rope_tables(jnp.arange(Lp), half, H)
    cos_s, sin_s = _rope_tables(jnp.tile(past_len + jnp.arange(Ls), nS), half, H)
    C = SAMPLE_PAD

    xp = x_prompt.reshape(nP * Lp, D)
    xs = x_sample.reshape(nS * Ls, D)
    memf = mem_prompt.reshape(nP * M, D)
    cache_kpe_t = jnp.swapaxes(cache_kpe, 2, 3)
    mem_k = cache_mem_k.reshape(depth, nS, M, D).astype(bf16)
    mem_v = cache_mem_v.reshape(depth, nS, M, D).astype(bf16)
    outs_p = [[] for _ in range(8)]
    outs_s = [[] for _ in range(6)]
    for l in range(depth):
        w = _prep_layer(l, params, dims)
        q, kv, ckv, kpe, g, mqkv, so, gates = _inproj(xp, cos_p, sin_p, w, dims=dims)
        va = _attn_out(_attn_prompt(q, kv, nP, dims=dims), w["wuv"]).astype(bf16)
        cb, nbuf = _conv_prompt(g, nP, w)
        hc, c1, n1, m1 = _mlstm_prompt(mqkv, so, gates, nP, w["ml_norm_g"], dims=dims)
        x1, qx = _merge(xp, va, cb, hc, w, alpha=alpha)
        mk, mv = _mem_kv(memf, w["xa_wk"], w["xa_wv"])
        ctx = _xattn_prompt(qx, mk, mv, nP, dims=dims)
        xp = _post(ctx, x1, w, alpha=alpha)
        for lst, val in zip(outs_p, (ckv.reshape(nP, Lp, kvl), kpe.reshape(nP, Lp, rope), nbuf, c1,
                                     n1[:, :, 0, :], m1[:, :, 0, 0], mk.reshape(nP, M, xheads, xhd),
                                     mv.reshape(nP, M, xheads, xhd))):
            lst.append(val)
        q, kv, ckv, kpe, g, mqkv, so, gates = _inproj(xs, cos_s, sin_s, w, dims=dims)
        q_s = q.reshape(H, nS, Ls, -1).transpose(1, 0, 2, 3).reshape(nS, H * Ls, -1)
        qpe = q_s[:, :, kvl:].reshape(nS, H, Ls, 2, H, half)
        qpe = qpe[:, jnp.arange(H), :, :, jnp.arange(H), :]
        qpe = qpe.transpose(1, 0, 2, 3, 4).reshape(nS, H * Ls, rope)
        kv_new = jnp.pad(kv.reshape(nS, Ls, -1), ((0, 0), (0, LANES - Ls), (0, 0)))
        o_s = _attn_sample(page_table, q_s, qpe, kv_new, cache_ckv, cache_kpe_t, l, dims=dims, n_new=Ls)
        o_s = o_s.reshape(nS, H, Ls, kvl).transpose(1, 0, 2, 3).reshape(H, nS * Ls, kvl).astype(bf16)
        va = _attn_out(o_s, w["wuv"]).astype(bf16)
        g_t = g.reshape(nS, Ls, cdim).transpose(1, 0, 2)
        cb_t, nbuf_t = _conv_sample(g_t, state_conv[l].transpose(1, 0, 2), w)
        cb = cb_t.transpose(1, 0, 2).reshape(nS * Ls, cdim)
        gates3 = _pad_tokens(gates.reshape(nS, Ls, -1), C)
        hg, c1, n1, m1 = _mlstm_sample(
            _pad_tokens(mqkv.reshape(nS, Ls, -1), C), _pad_tokens(so.reshape(nS, Ls, -1), C), gates3,
            gates3.transpose(0, 2, 1), state_C, state_n[l],
            jnp.broadcast_to(state_m[l][:, :, None], (nS, mheads, LANES)), l, w["ml_norm_g"], dims=dims, n_new=Ls)
        hc = hg[:, :Ls].reshape(nS * Ls, mdim)
        x1, qx = _merge(xs, va, cb, hc, w, alpha=alpha)
        ctx = _xattn_sample(_pad_tokens(qx.reshape(nS, Ls, D), C), mem_k, mem_v, l, dims=dims)
        xs = _post(ctx[:, :Ls].reshape(nS * Ls, D), x1, w, alpha=alpha)
        for lst, val in zip(outs_s, (ckv.reshape(nS, Ls, kvl), kpe.reshape(nS, Ls, rope),
                                     nbuf_t.transpose(1, 0, 2), c1, n1, m1[:, :, 0])):
            lst.append(val)
    return (xp.reshape(nP, Lp, D), xs.reshape(nS, Ls, D), *[jnp.stack(v) for v in outs_p],
            *[jnp.stack(v) for v in outs_s])
```

```python
import functools
import math

import jax
import jax.numpy as jnp
from jax import lax
from jax.experimental import pallas as pl
from jax.experimental.pallas import tpu as pltpu

f32 = jnp.float32
bf16 = jnp.bfloat16

LN_EPS = 1e-5
RMS_EPS = 1e-6
ROPE_BASE = 10000.0
LANES = 128
VMEM_LIMIT = 56 * 1024 * 1024
ML_CHUNK_PROMPT = 256
SAMPLE_PAD = 16
SAMPLE_PAGES_PER_STEP = 32
NEG_INF = float("-inf")
LOG2E = math.log2(math.e)
ATTN_KEY_BLOCK = 256
ATTN_QUERY_TILE = 512
ATTN_HEADS_PER_GROUP = 2


def _cparams(*sem):
    return pltpu.CompilerParams(dimension_semantics=sem, vmem_limit_bytes=VMEM_LIMIT)


def _tile(n, pref):
    t = min(n, pref)
    while n % t:
        t //= 2
    return t


def _dot(a, b):
    return jnp.dot(a.astype(bf16), b.astype(bf16), preferred_element_type=f32)


def _dot_nt(a, b):
    return lax.dot_general(a.astype(bf16), b.astype(bf16), (((1,), (1,)), ((), ())),
                           preferred_element_type=f32)


def _sigmoid(x):
    return 1.0 / (1.0 + jnp.exp(-x))


def _log_sigmoid(x):
    return jnp.minimum(x, 0.0) - jnp.log(1.0 + jnp.exp(-jnp.abs(x)))


def _layer_norm(x, g, b):
    mu = jnp.mean(x, -1, keepdims=True)
    xc = x - mu
    var = jnp.mean(xc * xc, -1, keepdims=True)
    return xc * lax.rsqrt(var + LN_EPS) * g + b


def _rms_norm(x, g):
    return x * lax.rsqrt(jnp.mean(x * x, -1, keepdims=True) + RMS_EPS) * g


def _full(shape):
    nd = len(shape)
    return pl.BlockSpec(shape, lambda *_: (0,) * nd)


def _inproj_kernel(x_ref, cos_ref, sin_ref, wa_ref, ba_ref, qg_ref, wuq_ref, kg_ref, wuk_ref,
                   wc_ref, bc_ref, wm_ref, bm_ref, wi_ref, bi_ref,
                   q_ref, kv_ref, ckv_ref, kpe_ref, g_ref, mqkv_ref, so_ref, gates_ref,
                   *, ql, kvl, heads, nope, half, cdim, mdim, k_scale):
    x = x_ref[...].astype(bf16)
    cos = cos_ref[...]
    sin = sin_ref[...]
    hp = heads * half

    za = jnp.dot(x, wa_ref[...], preferred_element_type=f32) + ba_ref[...]
    cq = _rms_norm(za[:, :ql], qg_ref[...])
    ckv = _rms_norm(za[:, ql:ql + kvl], kg_ref[...])
    k1 = za[:, ql + kvl:ql + kvl + hp]
    k2 = za[:, ql + kvl + hp:]
    k1r = k1 * cos - k2 * sin
    k2r = k2 * cos + k1 * sin
    ckv_ref[...] = ckv
    kpe_ref[...] = jnp.concatenate([k1r[:, :half], k2r[:, :half]], axis=-1)
    kv_ref[:, :kvl] = ckv.astype(bf16)
    kv_ref[:, kvl:kvl + hp] = k1r.astype(bf16)
    kv_ref[:, kvl + hp:] = k2r.astype(bf16)

    qq = _dot(cq, wuq_ref[...])
    hn = heads * nope
    q1 = qq[:, hn:hn + hp]
    q2 = qq[:, hn + hp:]
    q1r = q1 * cos - q2 * sin
    q2r = q2 * cos + q1 * sin
    lane_head = lax.broadcasted_iota(jnp.int32, (1, hp), 1) // half
    for h in range(heads):
        q_lat = _dot(qq[:, h * nope:(h + 1) * nope], wuk_ref[h])
        q_ref[h, :, :kvl] = q_lat.astype(bf16)
        mine = lane_head == h
        q_ref[h, :, kvl:kvl + hp] = jnp.where(mine, q1r, 0.0).astype(bf16)
        q_ref[h, :, kvl + hp:] = jnp.where(mine, q2r, 0.0).astype(bf16)

    u = jnp.dot(x, wc_ref[...], preferred_element_type=f32) + bc_ref[...]
    g_ref[...] = u[:, :cdim] * _sigmoid(u[:, cdim:])

    zm = jnp.dot(x, wm_ref[...], preferred_element_type=f32) + bm_ref[...]
    mqkv_ref[:, :mdim] = zm[:, :mdim].astype(bf16)
    mqkv_ref[:, mdim:2 * mdim] = (zm[:, mdim:2 * mdim] * k_scale).astype(bf16)
    mqkv_ref[:, 2 * mdim:] = zm[:, 2 * mdim:3 * mdim].astype(bf16)
    so_ref[...] = _sigmoid(zm[:, 3 * mdim:])
    zi = jnp.dot(x, wi_ref[...], preferred_element_type=f32) + bi_ref[...]
    gates_ref[...] = zi[:, :gates_ref.shape[1]]


def _inproj(x, cos_t, sin_t, w, *, dims):
    T, D = x.shape
    tm = _tile(T, 256)
    tab_blocks = cos_t.shape[0] // tm
    H, kvl, ql = dims["heads"], dims["kvl"], dims["ql"]
    hp = H * dims["half"]
    qk = kvl + 2 * hp
    cdim, mdim, ng = dims["cdim"], dims["mdim"], dims["ngates"]
    row = lambda i: (i, 0)
    tab = lambda i: (i % tab_blocks, 0)
    kern = functools.partial(_inproj_kernel, ql=ql, kvl=kvl, heads=H, nope=dims["nope"], half=dims["half"],
                             cdim=cdim, mdim=mdim, k_scale=dims["mhd"] ** -0.5)
    weights = (w["wa"], w["ba"], w["qg"], w["wuq"], w["kg"], w["wuk"], w["wc"], w["bc"], w["wm"], w["bm"],
               w["wi"], w["bi"])
    return pl.pallas_call(
        kern,
        grid=(T // tm,),
        in_specs=[pl.BlockSpec((tm, D), row), pl.BlockSpec((tm, hp), tab), pl.BlockSpec((tm, hp), tab)]
        + [_full(a.shape) for a in weights],
        out_specs=[pl.BlockSpec((H, tm, qk), lambda i: (0, i, 0)), pl.BlockSpec((tm, qk), row),
                   pl.BlockSpec((tm, kvl), row), pl.BlockSpec((tm, 2 * dims["half"]), row),
                   pl.BlockSpec((tm, cdim), row), pl.BlockSpec((tm, 3 * mdim), row),
                   pl.BlockSpec((tm, mdim), row), pl.BlockSpec((tm, ng), row)],
        out_shape=[jax.ShapeDtypeStruct((H, T, qk), bf16), jax.ShapeDtypeStruct((T, qk), bf16),
                   jax.ShapeDtypeStruct((T, kvl), f32), jax.ShapeDtypeStruct((T, 2 * dims["half"]), f32),
                   jax.ShapeDtypeStruct((T, cdim), f32), jax.ShapeDtypeStruct((T, 3 * mdim), bf16),
                   jax.ShapeDtypeStruct((T, mdim), f32), jax.ShapeDtypeStruct((T, ng), f32)],
        compiler_params=_cparams("parallel"),
        name="inproj",
    )(x, cos_t, sin_t, *weights)


def _attn_prompt_kernel(q_ref, kv_ref, wuv_ref, va_ref, m_ref, l_ref, acc_ref, *, tk, kvl, c_exp, hg):
    i = pl.program_id(1)
    heads, tq, qk = q_ref.shape
    nsub = tq // tk
    nchunk = tk // LANES
    nhg = heads // hg
    rows = hg * tk
    groups = [(h, t) for t in range(nsub) for h in range(nhg)]
    m_ref[...] = jnp.full(m_ref.shape, NEG_INF, f32)
    l_ref[...] = jnp.zeros(l_ref.shape, f32)
    acc_ref[...] = jnp.zeros(acc_ref.shape, f32)

    def run(blocks):
        items = [(j, diag, g) for (j, diag) in blocks for g in groups if diag is None or g[1] >= diag]

        def keys(j):
            return kv_ref[pl.ds(pl.multiple_of(j * tk, tk), tk), :]

        def scores(item):
            j, _, g = item
            q = q_ref[g[0] * hg:(g[0] + 1) * hg, g[1] * tk:(g[1] + 1) * tk, :]
            return _dot_nt(q.reshape(rows, qk), keys(j))

        s_next = scores(items[0])
        for idx, (j, diag, (h, t)) in enumerate(items):
            gi = t * nhg + h
            s = s_next
            if idx + 1 < len(items):
                s_next = scores(items[idx + 1])
            v = keys(j)[:, :kvl]
            if diag is not None and t == diag:
                qpos = lax.broadcasted_iota(jnp.int32, (rows, tk), 0) % tk
                kpos = lax.broadcasted_iota(jnp.int32, (rows, tk), 1)
                s = jnp.where(kpos <= qpos, s, NEG_INF)
            chunks = [s[:, c * LANES:(c + 1) * LANES] for c in range(nchunk)]
            m_prev = m_ref[gi]
            m_cur = jnp.max(functools.reduce(jnp.maximum, chunks), -1, keepdims=True)
            m_new = jnp.maximum(m_prev, m_cur)
            alpha = jnp.exp2((m_prev - m_new) * c_exp)
            ps = [jnp.exp2((ch - m_new) * c_exp) for ch in chunks]
            l_ref[gi] = alpha * l_ref[gi] + functools.reduce(jnp.add, ps)
            p = jnp.concatenate([x.astype(bf16) for x in ps], axis=-1)
            acc_ref[gi] = (jnp.concatenate([alpha] * (kvl // LANES), axis=-1) * acc_ref[gi]
                           + jnp.dot(p, v, preferred_element_type=f32))
            m_ref[gi] = m_new

    def body(jj, carry):
        run([(jj * nsub + u, None) for u in range(nsub)])
        return carry

    lax.fori_loop(0, i, body, 0)
    run([(i * nsub + d, d) for d in range(nsub)])
    vd = wuv_ref.shape[2]
    for (h, t) in groups:
        gi = t * nhg + h
        o = (acc_ref[gi] / jnp.sum(l_ref[gi], -1, keepdims=True)).astype(bf16)
        for u in range(hg):
            head = h * hg + u
            va_ref[t * tk:(t + 1) * tk, head * vd:(head + 1) * vd] = jnp.dot(
                o[u * tk:(u + 1) * tk, :], wuv_ref[head], preferred_element_type=f32).astype(va_ref.dtype)


def _attn_prompt(q, kv, wuv, n_seq, *, dims):
    H, T, qk = q.shape
    L = T // n_seq
    kvl = dims["kvl"]
    vd = wuv.shape[2]
    tk = _tile(L, ATTN_KEY_BLOCK)
    tq = _tile(L, ATTN_QUERY_TILE)
    nq = L // tq
    hg = math.gcd(H, ATTN_HEADS_PER_GROUP)
    ngroups = (H // hg) * (tq // tk)
    kern = functools.partial(_attn_prompt_kernel, tk=tk, kvl=kvl, c_exp=dims["attn_scale"] * LOG2E, hg=hg)
    return pl.pallas_call(
        kern,
        grid=(n_seq, nq),
        in_specs=[pl.BlockSpec((H, tq, qk), lambda n, i: (0, n * nq + i, 0)),
                  pl.BlockSpec((L, qk), lambda n, i: (n, 0)), _full(wuv.shape)],
        out_specs=pl.BlockSpec((tq, H * vd), lambda n, i: (n * nq + i, 0)),
        out_shape=jax.ShapeDtypeStruct((T, H * vd), bf16),
        scratch_shapes=[pltpu.VMEM((ngroups, hg * tk, LANES), f32), pltpu.VMEM((ngroups, hg * tk, LANES), f32),
                        pltpu.VMEM((ngroups, hg * tk, kvl), f32)],
        compiler_params=_cparams("parallel", "parallel"),
        name="attn_prompt",
    )(q, kv, wuv)


def _attn_sample_kernel(pt_ref, q_ref, qpe_ref, kvn_ref, ckv_hbm, kpe_hbm, o_ref,
                        ckv_buf, kpe_buf, sem, kbuf, m_ref, l_ref, acc_ref, *, layer, pages, kvl, c_exp, n_new):
    b = pl.program_id(0)
    j = pl.program_id(1)
    steps = pl.num_programs(1)
    step = b * steps + j
    slot = step % 2
    psz = ckv_buf.shape[2]

    def page_copies(bb, jj, sl):
        out = []
        for p in range(pages):
            pg = pt_ref[bb, jj * pages + p]
            out.append(pltpu.make_async_copy(ckv_hbm.at[layer, pg], ckv_buf.at[sl, p], sem.at[sl]))
            out.append(pltpu.make_async_copy(kpe_hbm.at[layer, pg], kpe_buf.at[sl, p], sem.at[sl]))
        return out

    @pl.when(step == 0)
    def _():
        for cp in page_copies(0, 0, 0):
            cp.start()

    @pl.when(step + 1 < pl.num_programs(0) * steps)
    def _():
        wrap = j + 1 == steps
        for cp in page_copies(jnp.where(wrap, b + 1, b), jnp.where(wrap, 0, j + 1), 1 - slot):
            cp.start()

    @pl.when(j == 0)
    def _():
        m_ref[...] = jnp.full(m_ref.shape, NEG_INF, f32)
        l_ref[...] = jnp.zeros(l_ref.shape, f32)
        acc_ref[...] = jnp.zeros(acc_ref.shape, f32)

    def online(s, v):
        m_prev = m_ref[...]
        m_new = jnp.maximum(m_prev, jnp.max(s, -1, keepdims=True))
        alpha = jnp.exp2((m_prev - m_new) * c_exp)
        p = jnp.exp2((s - m_new) * c_exp)
        l_ref[...] = alpha * l_ref[...] + jnp.sum(p, -1, keepdims=True)
        acc_ref[...] = alpha * acc_ref[...] + _dot(p, v)
        m_ref[...] = m_new

    for cp in page_copies(b, j, slot):
        cp.wait()
    for p in range(pages):
        kbuf[p * psz:(p + 1) * psz, :] = ckv_buf[slot, p].astype(bf16)
    kpe_t = jnp.concatenate([kpe_buf[slot, p].astype(bf16) for p in range(pages)], axis=-1)
    k = kbuf[...]
    online(_dot_nt(q_ref[:, :kvl], k) + jnp.dot(qpe_ref[...], kpe_t, preferred_element_type=f32), k)

    @pl.when(j == steps - 1)
    def _():
        kn = kvn_ref[...]
        s = _dot_nt(q_ref[...], kn)
        tok = lax.broadcasted_iota(jnp.int32, s.shape, 0) % n_new
        col = lax.broadcasted_iota(jnp.int32, s.shape, 1)
        s = jnp.where(col <= tok, s, NEG_INF)
        online(s, kn[:, :kvl])
        o_ref[...] = acc_ref[...] / l_ref[...]


def _attn_sample(page_table, q_s, qpe_s, kv_new, cache_ckv, cache_kpe_t, layer, *, dims, n_new):
    nb, rows, qk = q_s.shape
    n_pages = page_table.shape[1]
    psz = cache_ckv.shape[2]
    kvl, rope = dims["kvl"], 2 * dims["half"]
    pages = _tile(n_pages, SAMPLE_PAGES_PER_STEP)
    steps = n_pages // pages
    kern = functools.partial(_attn_sample_kernel, layer=layer, pages=pages, kvl=kvl,
                             c_exp=dims["attn_scale"] * LOG2E, n_new=n_new)
    grid_spec = pltpu.PrefetchScalarGridSpec(
        num_scalar_prefetch=1,
        grid=(nb, steps),
        in_specs=[pl.BlockSpec((None, rows, qk), lambda b, j, pt: (b, 0, 0)),
                  pl.BlockSpec((None, rows, rope), lambda b, j, pt: (b, 0, 0)),
                  pl.BlockSpec((None, LANES, qk), lambda b, j, pt: (b, 0, 0)),
                  pl.BlockSpec(memory_space=pl.ANY), pl.BlockSpec(memory_space=pl.ANY)],
        out_specs=pl.BlockSpec((None, rows, kvl), lambda b, j, pt: (b, 0, 0)),
        scratch_shapes=[pltpu.VMEM((2, pages, psz, kvl), f32), pltpu.VMEM((2, pages, rope, psz), f32),
                        pltpu.SemaphoreType.DMA((2,)), pltpu.VMEM((pages * psz, kvl), bf16),
                        pltpu.VMEM((rows, 1), f32), pltpu.VMEM((rows, 1), f32), pltpu.VMEM((rows, kvl), f32)],
    )
    return pl.pallas_call(
        kern,
        grid_spec=grid_spec,
        out_shape=jax.ShapeDtypeStruct((nb, rows, kvl), f32),
        compiler_params=_cparams("arbitrary", "arbitrary"),
        name="attn_sample",
    )(page_table, q_s, qpe_s, kv_new, cache_ckv, cache_kpe_t)


def _attn_out_kernel(o_ref, wuv_ref, va_ref):
    heads = o_ref.shape[0]
    vd = wuv_ref.shape[2]
    for h in range(heads):
        va_ref[:, h * vd:(h + 1) * vd] = jnp.dot(o_ref[h], wuv_ref[h],
                                                 preferred_element_type=f32).astype(va_ref.dtype)


def _attn_out(o, wuv):
    H, T, kvl = o.shape
    vd = wuv.shape[2]
    tm = _tile(T, 512)
    return pl.pallas_call(
        _attn_out_kernel,
        grid=(T // tm,),
        in_specs=[pl.BlockSpec((H, tm, kvl), lambda i: (0, i, 0)), _full(wuv.shape)],
        out_specs=pl.BlockSpec((tm, H * vd), lambda i: (i, 0)),
        out_shape=jax.ShapeDtypeStruct((T, H * vd), bf16),
        compiler_params=_cparams("parallel"),
        name="attn_out",
    )(o, wuv)


def _conv_prompt_kernel(g_ref, cw_ref, cb_ref, lg_ref, lb_ref, act_ref, buf_ref, xp_ref, sh_ref, *, taps, rc):
    i = pl.program_id(1)
    tm, cdim = g_ref.shape
    head = xp_ref.shape[0] - tm
    keep = taps - 1

    @pl.when(i == 0)
    def _():
        xp_ref[:head, :] = jnp.zeros((head, cdim), f32)

    xp_ref[head:, :] = g_ref[...]
    w = cw_ref[...]
    lead = head - keep
    span = sh_ref.shape[1]
    for r in range(1, 8):
        sh_ref[r - 1] = xp_ref[r:r + span, :]
    for c in range(tm // rc):
        acc = jnp.zeros((rc, cdim), f32)
        for t in range(taps):
            a, r = divmod(lead + t, 8)
            rows = slice(c * rc + 8 * a, c * rc + 8 * a + rc)
            acc = acc + (xp_ref[rows, :] if r == 0 else sh_ref[r - 1, rows, :]) * w[t:t + 1, :]
        y = _layer_norm(acc + cb_ref[...], lg_ref[...], lb_ref[...])
        act_ref[c * rc:(c + 1) * rc, :] = (y * _sigmoid(y)).astype(act_ref.dtype)

    @pl.when(i == pl.num_programs(1) - 1)
    def _():
        buf_ref[...] = xp_ref[head + tm - keep:, :]

    xp_ref[:head, :] = xp_ref[tm:, :]


def _conv_prompt(g, n_seq, w):
    T, cdim = g.shape
    L = T // n_seq
    taps = w["conv_w"].shape[0]
    tm = _tile(L, 256)
    nt = L // tm
    head = 32
    assert taps - 1 <= head <= tm
    kern = functools.partial(_conv_prompt_kernel, taps=taps, rc=_tile(tm, 32))
    return pl.pallas_call(
        kern,
        grid=(n_seq, nt),
        in_specs=[pl.BlockSpec((tm, cdim), lambda n, i: (n * nt + i, 0)), _full(w["conv_w"].shape),
                  _full(w["conv_b"].shape), _full(w["conv_ln_g"].shape), _full(w["conv_ln_b"].shape)],
        out_specs=[pl.BlockSpec((tm, cdim), lambda n, i: (n * nt + i, 0)),
                   pl.BlockSpec((None, taps - 1, cdim), lambda n, i: (n, 0, 0))],
        out_shape=[jax.ShapeDtypeStruct((T, cdim), bf16), jax.ShapeDtypeStruct((n_seq, taps - 1, cdim), f32)],
        scratch_shapes=[pltpu.VMEM((head + tm, cdim), f32), pltpu.VMEM((7, head + tm - 8, cdim), f32)],
        compiler_params=_cparams("parallel", "arbitrary"),
        name="conv_prompt",
    )(g, w["conv_w"], w["conv_b"], w["conv_ln_g"], w["conv_ln_b"])


def _conv_sample_kernel(g_ref, buf_ref, cw_ref, cb_ref, lg_ref, lb_ref, act_ref, nbuf_ref, *, taps):
    n_new = g_ref.shape[0]
    keep = taps - 1
    w = cw_ref[...]

    def xp(r):
        return buf_ref[r] if r < keep else g_ref[r - keep]

    for t in range(n_new):
        acc = xp(t) * w[0:1, :]
        for k in range(1, taps):
            acc = acc + xp(t + k) * w[k:k + 1, :]
        y = _layer_norm(acc + cb_ref[...], lg_ref[...], lb_ref[...])
        act_ref[t] = (y * _sigmoid(y)).astype(act_ref.dtype)
    for r in range(keep):
        nbuf_ref[r] = xp(r + n_new)


def _conv_sample(g_t, buf_t, w):
    n_new, nb, cdim = g_t.shape
    taps = w["conv_w"].shape[0]
    sb = _tile(nb, 32)
    kern = functools.partial(_conv_sample_kernel, taps=taps)
    return pl.pallas_call(
        kern,
        grid=(nb // sb,),
        in_specs=[pl.BlockSpec((n_new, sb, cdim), lambda i: (0, i, 0)),
                  pl.BlockSpec((taps - 1, sb, cdim), lambda i: (0, i, 0)), _full(w["conv_w"].shape),
                  _full(w["conv_b"].shape), _full(w["conv_ln_g"].shape), _full(w["conv_ln_b"].shape)],
        out_specs=[pl.BlockSpec((n_new, sb, cdim), lambda i: (0, i, 0)),
                   pl.BlockSpec((taps - 1, sb, cdim), lambda i: (0, i, 0))],
        out_shape=[jax.ShapeDtypeStruct((n_new, nb, cdim), bf16), jax.ShapeDtypeStruct((taps - 1, nb, cdim), f32)],
        compiler_params=_cparams("parallel"),
        name="conv_sample",
    )(g_t, buf_t, w["conv_w"], w["conv_b"], w["conv_ln_g"], w["conv_ln_b"])


def _head_norm_gate(h, g_row, so):
    mu = jnp.mean(h, -1, keepdims=True)
    hc = h - mu
    var = jnp.mean(hc * hc, -1, keepdims=True)
    return hc * lax.rsqrt(var + LN_EPS) * g_row * so


def _mlstm_chunk(q, k, v, ig_col, ig_row, lf_col, lf_row, m0, c0, n0, last):
    c = q.shape[0]
    row = lax.broadcasted_iota(jnp.int32, (c, c), 0)
    col = lax.broadcasted_iota(jnp.int32, (c, c), 1)
    causal = col <= row
    b_col = jnp.sum(jnp.where(causal, lf_row, 0.0), -1, keepdims=True)
    b_row = jnp.sum(jnp.where(row <= col, lf_col, 0.0), 0, keepdims=True)
    d = jnp.where(causal, b_col - b_row + ig_row, NEG_INF)
    inter = b_col + m0
    m = jnp.maximum(inter, jnp.max(d, -1, keepdims=True))
    dw = jnp.exp(d - m)
    iw = jnp.exp(inter - m)
    sw = _dot_nt(q, k) * dw
    num = iw * _dot(q, c0) + _dot(sw, v)
    den = iw * jnp.sum(q.astype(f32) * n0, -1, keepdims=True) + jnp.sum(sw, -1, keepdims=True)
    hh = num / jnp.maximum(jnp.abs(den), jnp.exp(-m))
    m_last = m[last:last + 1, :]
    b_last = b_col[last:last + 1, :]
    w_last = jnp.exp(b_last - b_col + ig_col - m_last)
    decay = jnp.exp(b_last + m0 - m_last)
    kw = w_last * k.astype(f32)
    c1 = decay * c0 + _dot(kw.T, v)
    n1 = decay * n0 + jnp.sum(kw, 0, keepdims=True)
    return hh, c1, n1, m_last


def _mlstm_prompt_kernel(qkv_ref, so_ref, gc_ref, gr_ref, ng_ref, hg_ref, c_out, n_out, m_out,
                         c_s, n_s, m_s, *, heads, hd):
    i = pl.program_id(1)
    c = qkv_ref.shape[0]
    mdim = heads * hd

    @pl.when(i == 0)
    def _():
        c_s[...] = jnp.zeros(c_s.shape, f32)
        n_s[...] = jnp.zeros(n_s.shape, f32)
        m_s[...] = jnp.zeros(m_s.shape, f32)

    gcol = gc_ref[...]
    grow = gr_ref[...]
    for h in range(heads):
        hh, c1, n1, m1 = _mlstm_chunk(
            qkv_ref[:, h * hd:(h + 1) * hd], qkv_ref[:, mdim + h * hd:mdim + (h + 1) * hd],
            qkv_ref[:, 2 * mdim + h * hd:2 * mdim + (h + 1) * hd],
            gcol[:, h:h + 1], grow[h:h + 1, :], _log_sigmoid(gcol[:, heads + h:heads + h + 1]),
            _log_sigmoid(grow[heads + h:heads + h + 1, :]), m_s[h][0:1, 0:1], c_s[h], n_s[h][0:1, :], c - 1)
        c_s[h] = c1
        n_s[h] = jnp.broadcast_to(n1, n_s.shape[1:])
        m_s[h] = jnp.broadcast_to(m1, m_s.shape[1:])
        hg_ref[:, h * hd:(h + 1) * hd] = _head_norm_gate(
            hh, ng_ref[:, h * hd:(h + 1) * hd], so_ref[:, h * hd:(h + 1) * hd]).astype(hg_ref.dtype)

    @pl.when(i == pl.num_programs(1) - 1)
    def _():
        c_out[...] = c_s[...]
        n_out[...] = n_s[...]
        m_out[...] = m_s[...]


def _mlstm_prompt(mqkv, so, gates, n_seq, ng, *, dims):
    T, _ = mqkv.shape
    L = T // n_seq
    heads, hd = dims["mheads"], dims["mhd"]
    mdim = heads * hd
    c = _tile(L, ML_CHUNK_PROMPT)
    nc = L // c
    ngates = gates.shape[1]
    gates_row = gates.reshape(n_seq * nc, c, ngates).transpose(0, 2, 1)
    kern = functools.partial(_mlstm_prompt_kernel, heads=heads, hd=hd)
    rowmap = lambda n, i: (n * nc + i, 0)
    return pl.pallas_call(
        kern,
        grid=(n_seq, nc),
        in_specs=[pl.BlockSpec((c, 3 * mdim), rowmap), pl.BlockSpec((c, mdim), rowmap),
                  pl.BlockSpec((c, ngates), rowmap),
                  pl.BlockSpec((None, ngates, c), lambda n, i: (n * nc + i, 0, 0)), _full(ng.shape)],
        out_specs=[pl.BlockSpec((c, mdim), rowmap),
                   pl.BlockSpec((None, heads, hd, hd), lambda n, i: (n, 0, 0, 0)),
                   pl.BlockSpec((None, heads, 8, hd), lambda n, i: (n, 0, 0, 0)),
                   pl.BlockSpec((None, heads, 8, LANES), lambda n, i: (n, 0, 0, 0))],
        out_shape=[jax.ShapeDtypeStruct((T, mdim), bf16), jax.ShapeDtypeStruct((n_seq, heads, hd, hd), f32),
                   jax.ShapeDtypeStruct((n_seq, heads, 8, hd), f32),
                   jax.ShapeDtypeStruct((n_seq, heads, 8, LANES), f32)],
        scratch_shapes=[pltpu.VMEM((heads, hd, hd), f32), pltpu.VMEM((heads, 8, hd), f32),
                        pltpu.VMEM((heads, 8, LANES), f32)],
        compiler_params=_cparams("parallel", "arbitrary"),
        name="mlstm_prompt",
    )(mqkv, so, gates, gates_row, ng)


def _mlstm_sample_kernel(qkv_ref, so_ref, gc_ref, gr_ref, c0_ref, n0_ref, m0_ref, ng_ref,
                         hg_ref, c_out, n_out, m_out, *, heads, hd, n_new):
    sb, c, _ = qkv_ref.shape
    mdim = heads * hd
    valid_col = lax.broadcasted_iota(jnp.int32, (c, 1), 0) < n_new
    valid_row = lax.broadcasted_iota(jnp.int32, (1, c), 1) < n_new

    def seq(b, carry):
        gcol = gc_ref[b]
        grow = gr_ref[b]
        for h in range(heads):
            hh, c1, n1, m1 = _mlstm_chunk(
                qkv_ref[b, :, h * hd:(h + 1) * hd], qkv_ref[b, :, mdim + h * hd:mdim + (h + 1) * hd],
                qkv_ref[b, :, 2 * mdim + h * hd:2 * mdim + (h + 1) * hd],
                jnp.where(valid_col, gcol[:, h:h + 1], NEG_INF), jnp.where(valid_row, grow[h:h + 1, :], NEG_INF),
                jnp.where(valid_col, _log_sigmoid(gcol[:, heads + h:heads + h + 1]), 0.0),
                jnp.where(valid_row, _log_sigmoid(grow[heads + h:heads + h + 1, :]), 0.0),
                m0_ref[b, h:h + 1, 0:1], c0_ref[b, h], n0_ref[b, h:h + 1, :], n_new - 1)
            c_out[b, h] = c1
            n_out[b, h:h + 1, :] = n1
            m_out[b, h:h + 1, :] = jnp.broadcast_to(m1, (1, m_out.shape[2]))
            hg_ref[b, :, h * hd:(h + 1) * hd] = _head_norm_gate(
                hh, ng_ref[:, h * hd:(h + 1) * hd], so_ref[b, :, h * hd:(h + 1) * hd]).astype(hg_ref.dtype)
        return carry

    lax.fori_loop(0, sb, seq, 0, unroll=2)


def _mlstm_sample(mqkv_p, so_p, gcol_p, grow_p, state_c, n0, m0b, layer, ng, *, dims, n_new):
    nb, c, _ = mqkv_p.shape
    heads, hd = dims["mheads"], dims["mhd"]
    mdim = heads * hd
    ngates = gcol_p.shape[2]
    sb = _tile(nb, 8)
    kern = functools.partial(_mlstm_sample_kernel, heads=heads, hd=hd, n_new=n_new)
    seqmap = lambda i: (i, 0, 0)
    return pl.pallas_call(
        kern,
        grid=(nb // sb,),
        in_specs=[pl.BlockSpec((sb, c, 3 * mdim), seqmap), pl.BlockSpec((sb, c, mdim), seqmap),
                  pl.BlockSpec((sb, c, ngates), seqmap), pl.BlockSpec((sb, ngates, c), seqmap),
                  pl.BlockSpec((None, sb, heads, hd, hd), lambda i: (layer, i, 0, 0, 0)),
                  pl.BlockSpec((sb, heads, hd), seqmap), pl.BlockSpec((sb, heads, LANES), seqmap),
                  _full(ng.shape)],
        out_specs=[pl.BlockSpec((sb, c, mdim), seqmap), pl.BlockSpec((sb, heads, hd, hd), lambda i: (i, 0, 0, 0)),
                   pl.BlockSpec((sb, heads, hd), seqmap), pl.BlockSpec((sb, heads, LANES), seqmap)],
        out_shape=[jax.ShapeDtypeStruct((nb, c, mdim), bf16), jax.ShapeDtypeStruct((nb, heads, hd, hd), f32),
                   jax.ShapeDtypeStruct((nb, heads, hd), f32), jax.ShapeDtypeStruct((nb, heads, LANES), f32)],
        compiler_params=_cparams("parallel"),
        name="mlstm_sample",
    )(mqkv_p, so_p, gcol_p, grow_p, state_c, n0, m0b, ng)


def _merge_kernel(x_ref, va_ref, cb_ref, hc_ref, wg_ref, bg_ref, woa_ref, wob_ref, woc_ref, wout_ref,
                  lg_ref, lb_ref, wq_ref, x1_ref, qx_ref, *, alpha):
    x = x_ref[...]
    xb = x.astype(bf16)
    d = x.shape[1]
    merged = jnp.zeros(x.shape, f32)
    for k, (b_ref, w_ref) in enumerate(((va_ref, woa_ref), (cb_ref, wob_ref), (hc_ref, woc_ref))):
        gate = _sigmoid(jnp.dot(xb, wg_ref[:, k * d:(k + 1) * d], preferred_element_type=f32)
                        + bg_ref[:, k * d:(k + 1) * d])
        merged = merged + gate * _dot(b_ref[...], w_ref[...])
    mix = _dot(merged, wout_ref[...])
    x1 = _layer_norm(alpha * x + mix, lg_ref[...], lb_ref[...])
    x1_ref[...] = x1
    qx_ref[...] = _dot(x1, wq_ref[...]).astype(qx_ref.dtype)


def _merge(x, va, cb, hc, w, *, alpha):
    T, D = x.shape
    tm = _tile(T, 256)
    row = lambda i: (i, 0)
    weights = (w["wg"], w["bg"], w["w_oa"], w["w_ob"], w["w_oc"], w["w_out"], w["ln1_g"], w["ln1_b"], w["xa_wq"])
    return pl.pallas_call(
        functools.partial(_merge_kernel, alpha=alpha),
        grid=(T // tm,),
        in_specs=[pl.BlockSpec((tm, D), row), pl.BlockSpec((tm, va.shape[1]), row),
                  pl.BlockSpec((tm, cb.shape[1]), row), pl.BlockSpec((tm, hc.shape[1]), row)]
        + [_full(a.shape) for a in weights],
        out_specs=[pl.BlockSpec((tm, D), row), pl.BlockSpec((tm, D), row)],
        out_shape=[jax.ShapeDtypeStruct((T, D), f32), jax.ShapeDtypeStruct((T, D), bf16)],
        compiler_params=_cparams("parallel"),
        name="merge",
    )(x, va, cb, hc, *weights)


def _mem_kv_kernel(mem_ref, wk_ref, wv_ref, mk_ref, mv_ref):
    mem = mem_ref[...].astype(bf16)
    mk_ref[...] = jnp.dot(mem, wk_ref[...], preferred_element_type=f32)
    mv_ref[...] = jnp.dot(mem, wv_ref[...], preferred_element_type=f32)


def _mem_kv(mem, wk, wv):
    R, D = mem.shape
    tm = _tile(R, 256)
    row = lambda i: (i, 0)
    return pl.pallas_call(
        _mem_kv_kernel,
        grid=(R // tm,),
        in_specs=[pl.BlockSpec((tm, D), row), _full(wk.shape), _full(wv.shape)],
        out_specs=[pl.BlockSpec((tm, D), row), pl.BlockSpec((tm, D), row)],
        out_shape=[jax.ShapeDtypeStruct((R, D), f32), jax.ShapeDtypeStruct((R, D), f32)],
        compiler_params=_cparams("parallel"),
        name="mem_kv",
    )(mem, wk, wv)


def _xattn_heads(q, mk, mv, heads, hd, scale):
    outs = []
    for h in range(heads):
        s = _dot_nt(q[:, h * hd:(h + 1) * hd], mk[:, h * hd:(h + 1) * hd]) * scale
        e = jnp.exp(s - jnp.max(s, -1, keepdims=True))
        p = e / jnp.sum(e, -1, keepdims=True)
        outs.append(_dot(p, mv[:, h * hd:(h + 1) * hd]))
    return outs


def _xattn_prompt_kernel(q_ref, mk_ref, mv_ref, ctx_ref, *, heads, hd, scale):
    outs = _xattn_heads(q_ref[...], mk_ref[...], mv_ref[...], heads, hd, scale)
    for h in range(heads):
        ctx_ref[:, h * hd:(h + 1) * hd] = outs[h].astype(ctx_ref.dtype)


def _xattn_prompt(qx, mk, mv, n_seq, *, dims):
    T, D = qx.shape
    L = T // n_seq
    M = mk.shape[0] // n_seq
    tm = _tile(L, 512)
    nt = L // tm
    heads, hd = dims["xheads"], dims["xhd"]
    kern = functools.partial(_xattn_prompt_kernel, heads=heads, hd=hd, scale=hd ** -0.5)
    return pl.pallas_call(
        kern,
        grid=(n_seq, nt),
        in_specs=[pl.BlockSpec((tm, D), lambda n, i: (n * nt + i, 0)), pl.BlockSpec((M, D), lambda n, i: (n, 0)),
                  pl.BlockSpec((M, D), lambda n, i: (n, 0))],
        out_specs=pl.BlockSpec((tm, D), lambda n, i: (n * nt + i, 0)),
        out_shape=jax.ShapeDtypeStruct((T, D), bf16),
        compiler_params=_cparams("parallel", "parallel"),
        name="xattn_prompt",
    )(qx, mk, mv)


def _xattn_sample_kernel(q_ref, mk_ref, mv_ref, ctx_ref, *, heads, hd, scale):
    sb = q_ref.shape[0]

    nj = hd // LANES
    mem = mk_ref.shape[1] // (nj * heads)

    def rows(j, h):
        return pl.ds(j * heads + h, mem, stride=nj * heads)

    def seq(b, carry):
        q = q_ref[b]
        for h in range(heads):
            s = _dot_nt(q[:, h * hd:h * hd + LANES], mk_ref[b, rows(0, h), :])
            for j in range(1, nj):
                s = s + _dot_nt(q[:, h * hd + j * LANES:h * hd + (j + 1) * LANES], mk_ref[b, rows(j, h), :])
            s = s * scale
            e = jnp.exp(s - jnp.max(s, -1, keepdims=True))
            p = e / jnp.sum(e, -1, keepdims=True)
            for j in range(nj):
                ctx_ref[b, :, h * hd + j * LANES:h * hd + (j + 1) * LANES] = _dot(
                    p, mv_ref[b, rows(j, h), :]).astype(ctx_ref.dtype)
        return carry

    lax.fori_loop(0, sb, seq, 0)


def _interleave_heads(mem):
    depth, n, M, heads, hd = mem.shape
    nj = hd // LANES
    return mem.reshape(depth, n, M, heads, nj, LANES).transpose(0, 1, 2, 4, 3, 5).reshape(
        depth, n, M * nj * heads, LANES)


def _xattn_sample(qx_p, mem_k, mem_v, layer, *, dims):
    nb, c, D = qx_p.shape
    R = mem_k.shape[2]
    sb = _tile(nb, 4)
    heads, hd = dims["xheads"], dims["xhd"]
    kern = functools.partial(_xattn_sample_kernel, heads=heads, hd=hd, scale=hd ** -0.5)
    memmap = lambda i: (layer, i, 0, 0)
    return pl.pallas_call(
        kern,
        grid=(nb // sb,),
        in_specs=[pl.BlockSpec((sb, c, D), lambda i: (i, 0, 0)), pl.BlockSpec((None, sb, R, LANES), memmap),
                  pl.BlockSpec((None, sb, R, LANES), memmap)],
        out_specs=pl.BlockSpec((sb, c, D), lambda i: (i, 0, 0)),
        out_shape=jax.ShapeDtypeStruct((nb, c, D), bf16),
        compiler_params=_cparams("parallel"),
        name="xattn_sample",
    )(qx_p, mem_k, mem_v)


def _post_kernel(ctx_ref, x1_ref, wo_ref, l2g_ref, l2b_ref, wup_ref, wdn_ref, l3g_ref, l3b_ref, x3_ref,
                 *, alpha, ff_chunk):
    x1 = x1_ref[...]
    x2 = _layer_norm(alpha * x1 + jnp.dot(ctx_ref[...], wo_ref[...], preferred_element_type=f32),
                     l2g_ref[...], l2b_ref[...])
    x2b = x2.astype(bf16)
    dff = wup_ref.shape[1]
    acc = jnp.zeros(x1.shape, f32)
    for c in range(dff // ff_chunk):
        hcol = jnp.maximum(jnp.dot(x2b, wup_ref[:, c * ff_chunk:(c + 1) * ff_chunk],
                                   preferred_element_type=f32), 0.0)
        acc = acc + _dot(hcol * hcol, wdn_ref[c * ff_chunk:(c + 1) * ff_chunk, :])
    x3_ref[...] = _layer_norm(alpha * x2 + acc, l3g_ref[...], l3b_ref[...])


def _post(ctx, x1, w, *, alpha):
    T, D = x1.shape
    tm = _tile(T, 256)
    row = lambda i: (i, 0)
    weights = (w["xa_wo"], w["ln2_g"], w["ln2_b"], w["w_up"], w["w_down"], w["ln3_g"], w["ln3_b"])
    return pl.pallas_call(
        functools.partial(_post_kernel, alpha=alpha, ff_chunk=_tile(w["w_up"].shape[1], 1024)),
        grid=(T // tm,),
        in_specs=[pl.BlockSpec((tm, D), row), pl.BlockSpec((tm, D), row)] + [_full(a.shape) for a in weights],
        out_specs=pl.BlockSpec((tm, D), row),
        out_shape=jax.ShapeDtypeStruct((T, D), f32),
        compiler_params=_cparams("parallel"),
        name="post",
    )(ctx, x1, *weights)


def _rope_tables(pos, half, heads):
    inv = ROPE_BASE ** (-jnp.arange(half, dtype=f32) / half)
    ang = pos.astype(f32)[:, None] * inv[None, :]
    return jnp.tile(jnp.cos(ang), (1, heads)), jnp.tile(jnp.sin(ang), (1, heads))


def _prep_layer(l, p, dims):
    ql, kvl, half, H, nope = dims["ql"], dims["kvl"], dims["half"], dims["heads"], dims["nope"]
    cdim, mdim, mheads = dims["cdim"], dims["mdim"], dims["mheads"]
    D = dims["d"]
    w_in, b_in = p["w_in"][l], p["b_in"][l]
    o = 0

    def take(n):
        nonlocal o
        cols = (w_in[:, o:o + n], b_in[o:o + n])
        o += n
        return cols

    (w_cq, b_cq), (w_kv, b_kv), (w_kp, b_kp) = take(ql), take(kvl), take(2 * half)
    (w_cu, b_cu) = take(2 * cdim)
    (w_m, b_m) = take(4 * mdim)
    (w_if, b_if) = take(2 * mheads)
    (w_g, b_g) = take(3 * D)
    tile_h = lambda a: jnp.tile(a, (1, H)) if a.ndim == 2 else jnp.tile(a, H)
    wa = jnp.concatenate([w_cq, w_kv, tile_h(w_kp[:, :half]), tile_h(w_kp[:, half:])], 1)
    ba = jnp.concatenate([b_cq, b_kv, tile_h(b_kp[:half]), tile_h(b_kp[half:])])
    w_uq = p["w_uq"][l]
    wuq = jnp.concatenate([w_uq[:, :, :nope].reshape(ql, H * nope),
                           w_uq[:, :, nope:nope + half].reshape(ql, H * half),
                           w_uq[:, :, nope + half:].reshape(ql, H * half)], 1)
    pad_if = LANES - 2 * mheads
    return {
        "wa": wa.astype(bf16), "ba": ba[None], "qg": p["q_norm_g"][l][None], "wuq": wuq.astype(bf16),
        "kg": p["kv_norm_g"][l][None], "wuk": p["w_uk"][l].transpose(1, 2, 0).astype(bf16),
        "wc": w_cu.astype(bf16), "bc": b_cu[None], "wm": w_m.astype(bf16), "bm": b_m[None],
        "wi": jnp.pad(w_if, ((0, 0), (0, pad_if))).astype(bf16), "bi": jnp.pad(b_if, (0, pad_if))[None],
        "wuv": p["w_uv"][l].transpose(1, 0, 2).astype(bf16),
        "conv_w": p["conv_w"][l], "conv_b": p["conv_b"][l][None], "conv_ln_g": p["conv_ln_g"][l][None],
        "conv_ln_b": p["conv_ln_b"][l][None],
        "ml_norm_g": p["ml_norm_g"][l].reshape(1, mdim),
        "wg": w_g.astype(bf16), "bg": b_g[None], "w_oa": p["w_oa"][l].astype(bf16),
        "w_ob": p["w_ob"][l].astype(bf16), "w_oc": p["w_oc"][l].astype(bf16), "w_out": p["w_out"][l].astype(bf16),
        "ln1_g": p["ln1_g"][l][None], "ln1_b": p["ln1_b"][l][None], "xa_wq": p["xa_wq"][l].astype(bf16),
        "xa_wk": p["xa_wk"][l].astype(bf16), "xa_wv": p["xa_wv"][l].astype(bf16),
        "xa_wo": p["xa_wo"][l].astype(bf16), "ln2_g": p["ln2_g"][l][None], "ln2_b": p["ln2_b"][l][None],
        "w_up": p["w_up"][l].astype(bf16), "w_down": p["w_down"][l].astype(bf16),
        "ln3_g": p["ln3_g"][l][None], "ln3_b": p["ln3_b"][l][None],
    }


def _pad_tokens(a, c):
    return jnp.pad(a, ((0, 0), (0, c - a.shape[1])) + ((0, 0),) * (a.ndim - 2))


def kernel(x_prompt, x_sample, mem_prompt, cache_ckv, cache_kpe, state_conv, state_C, state_n, state_m, cache_mem_k, cache_mem_v, page_table, w_in, b_in, q_norm_g, w_uq, kv_norm_g, w_uk, w_uv, w_oa, conv_w, conv_b, conv_ln_g, conv_ln_b, w_ob, ml_norm_g, w_oc, w_out, ln1_g, ln1_b, xa_wq, xa_wk, xa_wv, xa_wo, ln2_g, ln2_b, w_up, w_down, ln3_g, ln3_b):
    params = dict(w_in=w_in, b_in=b_in, q_norm_g=q_norm_g, w_uq=w_uq, kv_norm_g=kv_norm_g, w_uk=w_uk, w_uv=w_uv,
                  w_oa=w_oa, conv_w=conv_w, conv_b=conv_b, conv_ln_g=conv_ln_g, conv_ln_b=conv_ln_b, w_ob=w_ob,
                  ml_norm_g=ml_norm_g, w_oc=w_oc, w_out=w_out, ln1_g=ln1_g, ln1_b=ln1_b, xa_wq=xa_wq, xa_wk=xa_wk,
                  xa_wv=xa_wv, xa_wo=xa_wo, ln2_g=ln2_g, ln2_b=ln2_b, w_up=w_up, w_down=w_down, ln3_g=ln3_g,
                  ln3_b=ln3_b)
    depth, D, _ = w_in.shape
    nP, Lp, _ = x_prompt.shape
    nS, Ls, _ = x_sample.shape
    H, nope = w_uk.shape[2], w_uk.shape[3]
    rope = cache_kpe.shape[-1]
    half = rope // 2
    mheads, mhd = ml_norm_g.shape[1], ml_norm_g.shape[2]
    xheads, xhd = cache_mem_k.shape[-2], cache_mem_k.shape[-1]
    M = mem_prompt.shape[1]
    dims = dict(d=D, ql=q_norm_g.shape[1], kvl=kv_norm_g.shape[1], heads=H, nope=nope, half=half,
                cdim=conv_w.shape[2], mdim=mheads * mhd, mheads=mheads, mhd=mhd, ngates=2 * mheads,
                xheads=xheads, xhd=xhd, attn_scale=(nope + rope) ** -0.5)
    assert H * half == LANES and 2 * mheads <= LANES
    kvl, mdim, cdim, taps = dims["kvl"], dims["mdim"], dims["cdim"], conv_w.shape[1]
    alpha = (2 * depth) ** 0.25
    past_len = page_table.shape[1] * cache_ckv.shape[2]
    cos_p, sin_p = _rope_tables(jnp.arange(Lp), half, H)
    cos_s, sin_s = _rope_tables(jnp.tile(past_len + jnp.arange(Ls), nS), half, H)
    C = SAMPLE_PAD

    xp = x_prompt.reshape(nP * Lp, D)
    xs = x_sample.reshape(nS * Ls, D)
    memf = mem_prompt.reshape(nP * M, D)
    cache_kpe_t = jnp.swapaxes(cache_kpe, 2, 3)
    mem_k = _interleave_heads(cache_mem_k)
    mem_v = _interleave_heads(cache_mem_v)
    outs_p = [[] for _ in range(8)]
    outs_s = [[] for _ in range(6)]
    for l in range(depth):
        w = _prep_layer(l, params, dims)
        q, kv, ckv, kpe, g, mqkv, so, gates = _inproj(xs, cos_s, sin_s, w, dims=dims)
        q_s = q.reshape(H, nS, Ls, -1).transpose(1, 0, 2, 3).reshape(nS, H * Ls, -1)
        qpe = q_s[:, :, kvl:].reshape(nS, H, Ls, 2, H, half)
        qpe = qpe[:, jnp.arange(H), :, :, jnp.arange(H), :]
        qpe = qpe.transpose(1, 0, 2, 3, 4).reshape(nS, H * Ls, rope)
        kv_new = jnp.pad(kv.reshape(nS, Ls, -1), ((0, 0), (0, LANES - Ls), (0, 0)))
        o_s = _attn_sample(page_table, q_s, qpe, kv_new, cache_ckv, cache_kpe_t, l, dims=dims, n_new=Ls)
        o_s = o_s.reshape(nS, H, Ls, kvl).transpose(1, 0, 2, 3).reshape(H, nS * Ls, kvl).astype(bf16)
        va = _attn_out(o_s, w["wuv"])
        g_t = g.reshape(nS, Ls, cdim).transpose(1, 0, 2)
        cb_t, nbuf_t = _conv_sample(g_t, state_conv[l].transpose(1, 0, 2), w)
        cb = cb_t.transpose(1, 0, 2).reshape(nS * Ls, cdim)
        gates3 = _pad_tokens(gates.reshape(nS, Ls, -1), C)
        hg, c1, n1, m1 = _mlstm_sample(
            _pad_tokens(mqkv.reshape(nS, Ls, -1), C), _pad_tokens(so.reshape(nS, Ls, -1), C), gates3,
            gates3.transpose(0, 2, 1), state_C, state_n[l],
            jnp.broadcast_to(state_m[l][:, :, None], (nS, mheads, LANES)), l, w["ml_norm_g"], dims=dims, n_new=Ls)
        hc = hg[:, :Ls].reshape(nS * Ls, mdim)
        x1, qx = _merge(xs, va, cb, hc, w, alpha=alpha)
        ctx = _xattn_sample(_pad_tokens(qx.reshape(nS, Ls, D), C), mem_k, mem_v, l, dims=dims)
        xs = _post(ctx[:, :Ls].reshape(nS * Ls, D), x1, w, alpha=alpha)
        for lst, val in zip(outs_s, (ckv.reshape(nS, Ls, kvl), kpe.reshape(nS, Ls, rope),
                                     nbuf_t.transpose(1, 0, 2), c1, n1, m1[:, :, 0])):
            lst.append(val)
        q, kv, ckv, kpe, g, mqkv, so, gates = _inproj(xp, cos_p, sin_p, w, dims=dims)
        va = _attn_prompt(q, kv, w["wuv"], nP, dims=dims)
        cb, nbuf = _conv_prompt(g, nP, w)
        hc, c1, n1, m1 = _mlstm_prompt(mqkv, so, gates, nP, w["ml_norm_g"], dims=dims)
        x1, qx = _merge(xp, va, cb, hc, w, alpha=alpha)
        mk, mv = _mem_kv(memf, w["xa_wk"], w["xa_wv"])
        ctx = _xattn_prompt(qx, mk, mv, nP, dims=dims)
        xp = _post(ctx, x1, w, alpha=alpha)
        for lst, val in zip(outs_p, (ckv.reshape(nP, Lp, kvl), kpe.reshape(nP, Lp, rope), nbuf, c1,
                                     n1[:, :, 0, :], m1[:, :, 0, 0], mk.reshape(nP, M, xheads, xhd),
                                     mv.reshape(nP, M, xheads, xhd))):
            lst.append(val)
    return (xp.reshape(nP, Lp, D), xs.reshape(nS, Ls, D), *[jnp.stack(v) for v in outs_p],
            *[jnp.stack(v) for v in outs_s])
```

```python
import functools
import math

import jax
import jax.numpy as jnp
from jax import lax
from jax.experimental import pallas as pl
from jax.experimental.pallas import tpu as pltpu

f32 = jnp.float32
bf16 = jnp.bfloat16

LN_EPS = 1e-5
RMS_EPS = 1e-6
ROPE_BASE = 10000.0
LANES = 128
VMEM_LIMIT = 56 * 1024 * 1024
ML_CHUNK_PROMPT = 256
SAMPLE_PAD = 16
SAMPLE_PAGES_PER_STEP = 32
NEG_INF = float("-inf")
LOG2E = math.log2(math.e)
ROW_TILE = 512
ATTN_KEY_BLOCK = 256
ATTN_QUERY_TILE = 512
ATTN_TILES_PER_TRIP = 2
ATTN_HEADS_PER_GROUP = 2


def _cparams(*sem):
    return pltpu.CompilerParams(dimension_semantics=sem, vmem_limit_bytes=VMEM_LIMIT)


def _tile(n, pref):
    t = min(n, pref)
    while n % t:
        t //= 2
    return t


def _dot(a, b):
    return jnp.dot(a.astype(bf16), b.astype(bf16), preferred_element_type=f32)


def _dot_nt(a, b):
    return lax.dot_general(a.astype(bf16), b.astype(bf16), (((1,), (1,)), ((), ())),
                           preferred_element_type=f32)


def _sigmoid(x):
    return 1.0 / (1.0 + jnp.exp(-x))


def _log_sigmoid(x):
    return jnp.minimum(x, 0.0) - jnp.log(1.0 + jnp.exp(-jnp.abs(x)))


def _layer_norm(x, g, b):
    mu = jnp.mean(x, -1, keepdims=True)
    xc = x - mu
    var = jnp.mean(xc * xc, -1, keepdims=True)
    return xc * lax.rsqrt(var + LN_EPS) * g + b


def _rms_norm(x, g):
    return x * lax.rsqrt(jnp.mean(x * x, -1, keepdims=True) + RMS_EPS) * g


def _full(shape):
    nd = len(shape)
    return pl.BlockSpec(shape, lambda *_: (0,) * nd)


def _inproj_kernel(x_ref, cos_ref, sin_ref, wa_ref, ba_ref, qg_ref, wuq_ref, kg_ref, wuk_ref,
                   wc_ref, bc_ref, wm_ref, bm_ref, wi_ref, bi_ref,
                   q_ref, kv_ref, ckv_ref, kpe_ref, g_ref, mqkv_ref, so_ref, gates_ref,
                   *, ql, kvl, heads, nope, half, cdim, mdim, k_scale):
    x = x_ref[...].astype(bf16)
    cos = cos_ref[...]
    sin = sin_ref[...]
    hp = heads * half

    za = jnp.dot(x, wa_ref[...], preferred_element_type=f32) + ba_ref[...]
    cq = _rms_norm(za[:, :ql], qg_ref[...])
    ckv = _rms_norm(za[:, ql:ql + kvl], kg_ref[...])
    k1 = za[:, ql + kvl:ql + kvl + hp]
    k2 = za[:, ql + kvl + hp:]
    k1r = k1 * cos - k2 * sin
    k2r = k2 * cos + k1 * sin
    ckv_ref[...] = ckv
    kpe_ref[...] = jnp.concatenate([k1r[:, :half], k2r[:, :half]], axis=-1)
    kv_ref[:, :kvl] = ckv.astype(bf16)
    kv_ref[:, kvl:kvl + hp] = k1r.astype(bf16)
    kv_ref[:, kvl + hp:] = k2r.astype(bf16)

    qq = _dot(cq, wuq_ref[...])
    hn = heads * nope
    q1 = qq[:, hn:hn + hp]
    q2 = qq[:, hn + hp:]
    q1r = q1 * cos - q2 * sin
    q2r = q2 * cos + q1 * sin
    lane_head = lax.broadcasted_iota(jnp.int32, (1, hp), 1) // half
    for h in range(heads):
        q_lat = _dot(qq[:, h * nope:(h + 1) * nope], wuk_ref[h])
        q_ref[h, :, :kvl] = q_lat.astype(bf16)
        mine = lane_head == h
        q_ref[h, :, kvl:kvl + hp] = jnp.where(mine, q1r, 0.0).astype(bf16)
        q_ref[h, :, kvl + hp:] = jnp.where(mine, q2r, 0.0).astype(bf16)

    u = jnp.dot(x, wc_ref[...], preferred_element_type=f32) + bc_ref[...]
    g_ref[...] = u[:, :cdim] * _sigmoid(u[:, cdim:])

    zm = jnp.dot(x, wm_ref[...], preferred_element_type=f32) + bm_ref[...]
    mqkv_ref[:, :mdim] = zm[:, :mdim].astype(bf16)
    mqkv_ref[:, mdim:2 * mdim] = (zm[:, mdim:2 * mdim] * k_scale).astype(bf16)
    mqkv_ref[:, 2 * mdim:] = zm[:, 2 * mdim:3 * mdim].astype(bf16)
    so_ref[...] = _sigmoid(zm[:, 3 * mdim:])
    zi = jnp.dot(x, wi_ref[...], preferred_element_type=f32) + bi_ref[...]
    gates_ref[...] = zi[:, :gates_ref.shape[1]]


def _inproj(x, cos_t, sin_t, w, *, dims):
    T, D = x.shape
    tm = _tile(T, ROW_TILE)
    tab_blocks = cos_t.shape[0] // tm
    H, kvl, ql = dims["heads"], dims["kvl"], dims["ql"]
    hp = H * dims["half"]
    qk = kvl + 2 * hp
    cdim, mdim, ng = dims["cdim"], dims["mdim"], dims["ngates"]
    row = lambda i: (i, 0)
    tab = lambda i: (i % tab_blocks, 0)
    kern = functools.partial(_inproj_kernel, ql=ql, kvl=kvl, heads=H, nope=dims["nope"], half=dims["half"],
                             cdim=cdim, mdim=mdim, k_scale=dims["mhd"] ** -0.5)
    weights = (w["wa"], w["ba"], w["qg"], w["wuq"], w["kg"], w["wuk"], w["wc"], w["bc"], w["wm"], w["bm"],
               w["wi"], w["bi"])
    return pl.pallas_call(
        kern,
        grid=(T // tm,),
        in_specs=[pl.BlockSpec((tm, D), row), pl.BlockSpec((tm, hp), tab), pl.BlockSpec((tm, hp), tab)]
        + [_full(a.shape) for a in weights],
        out_specs=[pl.BlockSpec((H, tm, qk), lambda i: (0, i, 0)), pl.BlockSpec((tm, qk), row),
                   pl.BlockSpec((tm, kvl), row), pl.BlockSpec((tm, 2 * dims["half"]), row),
                   pl.BlockSpec((tm, cdim), row), pl.BlockSpec((tm, 3 * mdim), row),
                   pl.BlockSpec((tm, mdim), row), pl.BlockSpec((tm, ng), row)],
        out_shape=[jax.ShapeDtypeStruct((H, T, qk), bf16), jax.ShapeDtypeStruct((T, qk), bf16),
                   jax.ShapeDtypeStruct((T, kvl), f32), jax.ShapeDtypeStruct((T, 2 * dims["half"]), f32),
                   jax.ShapeDtypeStruct((T, cdim), f32), jax.ShapeDtypeStruct((T, 3 * mdim), bf16),
                   jax.ShapeDtypeStruct((T, mdim), f32), jax.ShapeDtypeStruct((T, ng), f32)],
        compiler_params=_cparams("parallel"),
        name="inproj",
    )(x, cos_t, sin_t, *weights)


def _attn_prompt_kernel(q_ref, kv_ref, wuv_ref, va_ref, m_ref, l_ref, acc_ref, *, tk, kvl, c_exp, hg):
    i = pl.program_id(1)
    heads, tq, qk = q_ref.shape
    nsub = tq // tk
    nchunk = tk // LANES
    nhg = heads // hg
    rows = hg * tk
    groups = [(h, t) for t in range(nsub) for h in range(nhg)]
    m_ref[...] = jnp.full(m_ref.shape, NEG_INF, f32)
    l_ref[...] = jnp.zeros(l_ref.shape, f32)
    acc_ref[...] = jnp.zeros(acc_ref.shape, f32)

    def run(blocks):
        items = [(j, diag, g) for (j, diag) in blocks for g in groups if diag is None or g[1] >= diag]

        def keys(j):
            return kv_ref[pl.ds(pl.multiple_of(j * tk, tk), tk), :]

        def scores(item):
            j, _, g = item
            q = q_ref[g[0] * hg:(g[0] + 1) * hg, g[1] * tk:(g[1] + 1) * tk, :]
            return _dot_nt(q.reshape(rows, qk), keys(j))

        s_next = scores(items[0])
        for idx, (j, diag, (h, t)) in enumerate(items):
            gi = t * nhg + h
            s = s_next
            if idx + 1 < len(items):
                s_next = scores(items[idx + 1])
            v = keys(j)[:, :kvl]
            if diag is not None and t == diag:
                qpos = lax.broadcasted_iota(jnp.int32, (rows, tk), 0) % tk
                kpos = lax.broadcasted_iota(jnp.int32, (rows, tk), 1)
                s = jnp.where(kpos <= qpos, s, NEG_INF)
            chunks = [s[:, c * LANES:(c + 1) * LANES] for c in range(nchunk)]
            m_prev = m_ref[gi]
            m_cur = jnp.max(functools.reduce(jnp.maximum, chunks), -1, keepdims=True)
            m_new = jnp.maximum(m_prev, m_cur)
            alpha = jnp.exp2((m_prev - m_new) * c_exp)
            ps = [jnp.exp2((ch - m_new) * c_exp) for ch in chunks]
            l_ref[gi] = alpha * l_ref[gi] + functools.reduce(jnp.add, ps)
            p = jnp.concatenate([x.astype(bf16) for x in ps], axis=-1)
            acc_ref[gi] = (jnp.concatenate([alpha] * (kvl // LANES), axis=-1) * acc_ref[gi]
                           + jnp.dot(p, v, preferred_element_type=f32))
            m_ref[gi] = m_new

    per_trip = ATTN_TILES_PER_TRIP * nsub

    def body(jj, carry):
        run([(jj * per_trip + u, None) for u in range(per_trip)])
        return carry

    lax.fori_loop(0, i // ATTN_TILES_PER_TRIP, body, 0)
    for r in range(ATTN_TILES_PER_TRIP - 1):

        @pl.when(i % ATTN_TILES_PER_TRIP > r)
        def _(r=r):
            first = ((i // ATTN_TILES_PER_TRIP) * ATTN_TILES_PER_TRIP + r) * nsub
            run([(first + u, None) for u in range(nsub)])

    run([(i * nsub + d, d) for d in range(nsub)])
    vd = wuv_ref.shape[2]
    for (h, t) in groups:
        gi = t * nhg + h
        o = (acc_ref[gi] / jnp.sum(l_ref[gi], -1, keepdims=True)).astype(bf16)
        for u in range(hg):
            head = h * hg + u
            va_ref[t * tk:(t + 1) * tk, head * vd:(head + 1) * vd] = jnp.dot(
                o[u * tk:(u + 1) * tk, :], wuv_ref[head], preferred_element_type=f32).astype(va_ref.dtype)


def _attn_prompt(q, kv, wuv, n_seq, *, dims):
    H, T, qk = q.shape
    L = T // n_seq
    kvl = dims["kvl"]
    vd = wuv.shape[2]
    tk = _tile(L, ATTN_KEY_BLOCK)
    tq = _tile(L, ATTN_QUERY_TILE)
    nq = L // tq
    hg = math.gcd(H, ATTN_HEADS_PER_GROUP)
    ngroups = (H // hg) * (tq // tk)
    kern = functools.partial(_attn_prompt_kernel, tk=tk, kvl=kvl, c_exp=dims["attn_scale"] * LOG2E, hg=hg)
    return pl.pallas_call(
        kern,
        grid=(n_seq, nq),
        in_specs=[pl.BlockSpec((H, tq, qk), lambda n, i: (0, n * nq + i, 0)),
                  pl.BlockSpec((L, qk), lambda n, i: (n, 0)), _full(wuv.shape)],
        out_specs=pl.BlockSpec((tq, H * vd), lambda n, i: (n * nq + i, 0)),
        out_shape=jax.ShapeDtypeStruct((T, H * vd), bf16),
        scratch_shapes=[pltpu.VMEM((ngroups, hg * tk, LANES), f32), pltpu.VMEM((ngroups, hg * tk, LANES), f32),
                        pltpu.VMEM((ngroups, hg * tk, kvl), f32)],
        compiler_params=_cparams("parallel", "parallel"),
        name="attn_prompt",
    )(q, kv, wuv)


def _attn_sample_kernel(pt_ref, q_ref, qpe_ref, kvn_ref, ckv_hbm, kpe_hbm, o_ref,
                        ckv_buf, kpe_buf, sem, kbuf, m_ref, l_ref, acc_ref, *, layer, pages, kvl, c_exp, n_new):
    b = pl.program_id(0)
    j = pl.program_id(1)
    steps = pl.num_programs(1)
    step = b * steps + j
    slot = step % 2
    psz = ckv_buf.shape[2]

    def page_copies(bb, jj, sl):
        out = []
        for p in range(pages):
            pg = pt_ref[bb, jj * pages + p]
            out.append(pltpu.make_async_copy(ckv_hbm.at[layer, pg], ckv_buf.at[sl, p], sem.at[sl]))
            out.append(pltpu.make_async_copy(kpe_hbm.at[layer, pg], kpe_buf.at[sl, p], sem.at[sl]))
        return out

    @pl.when(step == 0)
    def _():
        for cp in page_copies(0, 0, 0):
            cp.start()

    @pl.when(step + 1 < pl.num_programs(0) * steps)
    def _():
        wrap = j + 1 == steps
        for cp in page_copies(jnp.where(wrap, b + 1, b), jnp.where(wrap, 0, j + 1), 1 - slot):
            cp.start()

    @pl.when(j == 0)
    def _():
        m_ref[...] = jnp.full(m_ref.shape, NEG_INF, f32)
        l_ref[...] = jnp.zeros(l_ref.shape, f32)
        acc_ref[...] = jnp.zeros(acc_ref.shape, f32)

    def online(s, v):
        m_prev = m_ref[...]
        m_new = jnp.maximum(m_prev, jnp.max(s, -1, keepdims=True))
        alpha = jnp.exp2((m_prev - m_new) * c_exp)
        p = jnp.exp2((s - m_new) * c_exp)
        l_ref[...] = alpha * l_ref[...] + jnp.sum(p, -1, keepdims=True)
        acc_ref[...] = alpha * acc_ref[...] + _dot(p, v)
        m_ref[...] = m_new

    for cp in page_copies(b, j, slot):
        cp.wait()
    for p in range(pages):
        kbuf[p * psz:(p + 1) * psz, :] = ckv_buf[slot, p].astype(bf16)
    kpe_t = jnp.concatenate([kpe_buf[slot, p].astype(bf16) for p in range(pages)], axis=-1)
    k = kbuf[...]
    online(_dot_nt(q_ref[:, :kvl], k) + jnp.dot(qpe_ref[...], kpe_t, preferred_element_type=f32), k)

    @pl.when(j == steps - 1)
    def _():
        kn = kvn_ref[...]
        s = _dot_nt(q_ref[...], kn)
        tok = lax.broadcasted_iota(jnp.int32, s.shape, 0) % n_new
        col = lax.broadcasted_iota(jnp.int32, s.shape, 1)
        s = jnp.where(col <= tok, s, NEG_INF)
        online(s, kn[:, :kvl])
        o_ref[...] = acc_ref[...] / l_ref[...]


def _attn_sample(page_table, q_s, qpe_s, kv_new, cache_ckv, cache_kpe_t, layer, *, dims, n_new):
    nb, rows, qk = q_s.shape
    n_pages = page_table.shape[1]
    psz = cache_ckv.shape[2]
    kvl, rope = dims["kvl"], 2 * dims["half"]
    pages = _tile(n_pages, SAMPLE_PAGES_PER_STEP)
    steps = n_pages // pages
    kern = functools.partial(_attn_sample_kernel, layer=layer, pages=pages, kvl=kvl,
                             c_exp=dims["attn_scale"] * LOG2E, n_new=n_new)
    grid_spec = pltpu.PrefetchScalarGridSpec(
        num_scalar_prefetch=1,
        grid=(nb, steps),
        in_specs=[pl.BlockSpec((None, rows, qk), lambda b, j, pt: (b, 0, 0)),
                  pl.BlockSpec((None, rows, rope), lambda b, j, pt: (b, 0, 0)),
                  pl.BlockSpec((None, LANES, qk), lambda b, j, pt: (b, 0, 0)),
                  pl.BlockSpec(memory_space=pl.ANY), pl.BlockSpec(memory_space=pl.ANY)],
        out_specs=pl.BlockSpec((None, rows, kvl), lambda b, j, pt: (b, 0, 0)),
        scratch_shapes=[pltpu.VMEM((2, pages, psz, kvl), f32), pltpu.VMEM((2, pages, rope, psz), f32),
                        pltpu.SemaphoreType.DMA((2,)), pltpu.VMEM((pages * psz, kvl), bf16),
                        pltpu.VMEM((rows, 1), f32), pltpu.VMEM((rows, 1), f32), pltpu.VMEM((rows, kvl), f32)],
    )
    return pl.pallas_call(
        kern,
        grid_spec=grid_spec,
        out_shape=jax.ShapeDtypeStruct((nb, rows, kvl), f32),
        compiler_params=_cparams("arbitrary", "arbitrary"),
        name="attn_sample",
    )(page_table, q_s, qpe_s, kv_new, cache_ckv, cache_kpe_t)


def _attn_out_kernel(o_ref, wuv_ref, va_ref):
    heads = o_ref.shape[0]
    vd = wuv_ref.shape[2]
    for h in range(heads):
        va_ref[:, h * vd:(h + 1) * vd] = jnp.dot(o_ref[h], wuv_ref[h],
                                                 preferred_element_type=f32).astype(va_ref.dtype)


def _attn_out(o, wuv):
    H, T, kvl = o.shape
    vd = wuv.shape[2]
    tm = _tile(T, 512)
    return pl.pallas_call(
        _attn_out_kernel,
        grid=(T // tm,),
        in_specs=[pl.BlockSpec((H, tm, kvl), lambda i: (0, i, 0)), _full(wuv.shape)],
        out_specs=pl.BlockSpec((tm, H * vd), lambda i: (i, 0)),
        out_shape=jax.ShapeDtypeStruct((T, H * vd), bf16),
        compiler_params=_cparams("parallel"),
        name="attn_out",
    )(o, wuv)


def _conv_prompt_kernel(g_ref, cw_ref, cb_ref, lg_ref, lb_ref, act_ref, buf_ref, xp_ref, sh_ref, *, taps, rc):
    i = pl.program_id(1)
    tm, cdim = g_ref.shape
    head = xp_ref.shape[0] - tm
    keep = taps - 1

    @pl.when(i == 0)
    def _():
        xp_ref[:head, :] = jnp.zeros((head, cdim), f32)

    xp_ref[head:, :] = g_ref[...]
    w = cw_ref[...]
    lead = head - keep
    span = sh_ref.shape[1]
    for r in range(1, 8):
        sh_ref[r - 1] = xp_ref[r:r + span, :]
    for c in range(tm // rc):
        acc = jnp.zeros((rc, cdim), f32)
        for t in range(taps):
            a, r = divmod(lead + t, 8)
            rows = slice(c * rc + 8 * a, c * rc + 8 * a + rc)
            acc = acc + (xp_ref[rows, :] if r == 0 else sh_ref[r - 1, rows, :]) * w[t:t + 1, :]
        y = _layer_norm(acc + cb_ref[...], lg_ref[...], lb_ref[...])
        act_ref[c * rc:(c + 1) * rc, :] = (y * _sigmoid(y)).astype(act_ref.dtype)

    @pl.when(i == pl.num_programs(1) - 1)
    def _():
        buf_ref[...] = xp_ref[head + tm - keep:, :]

    xp_ref[:head, :] = xp_ref[tm:, :]


def _conv_prompt(g, n_seq, w):
    T, cdim = g.shape
    L = T // n_seq
    taps = w["conv_w"].shape[0]
    tm = _tile(L, 256)
    nt = L // tm
    head = 32
    assert taps - 1 <= head <= tm
    kern = functools.partial(_conv_prompt_kernel, taps=taps, rc=_tile(tm, 32))
    return pl.pallas_call(
        kern,
        grid=(n_seq, nt),
        in_specs=[pl.BlockSpec((tm, cdim), lambda n, i: (n * nt + i, 0)), _full(w["conv_w"].shape),
                  _full(w["conv_b"].shape), _full(w["conv_ln_g"].shape), _full(w["conv_ln_b"].shape)],
        out_specs=[pl.BlockSpec((tm, cdim), lambda n, i: (n * nt + i, 0)),
                   pl.BlockSpec((None, taps - 1, cdim), lambda n, i: (n, 0, 0))],
        out_shape=[jax.ShapeDtypeStruct((T, cdim), bf16), jax.ShapeDtypeStruct((n_seq, taps - 1, cdim), f32)],
        scratch_shapes=[pltpu.VMEM((head + tm, cdim), f32), pltpu.VMEM((7, head + tm - 8, cdim), f32)],
        compiler_params=_cparams("parallel", "arbitrary"),
        name="conv_prompt",
    )(g, w["conv_w"], w["conv_b"], w["conv_ln_g"], w["conv_ln_b"])


def _conv_sample_kernel(g_ref, buf_ref, cw_ref, cb_ref, lg_ref, lb_ref, act_ref, nbuf_ref, *, taps):
    n_new = g_ref.shape[0]
    keep = taps - 1
    w = cw_ref[...]

    def xp(r):
        return buf_ref[r] if r < keep else g_ref[r - keep]

    for t in range(n_new):
        acc = xp(t) * w[0:1, :]
        for k in range(1, taps):
            acc = acc + xp(t + k) * w[k:k + 1, :]
        y = _layer_norm(acc + cb_ref[...], lg_ref[...], lb_ref[...])
        act_ref[t] = (y * _sigmoid(y)).astype(act_ref.dtype)
    for r in range(keep):
        nbuf_ref[r] = xp(r + n_new)


def _conv_sample(g_t, buf_t, w):
    n_new, nb, cdim = g_t.shape
    taps = w["conv_w"].shape[0]
    sb = _tile(nb, 32)
    kern = functools.partial(_conv_sample_kernel, taps=taps)
    return pl.pallas_call(
        kern,
        grid=(nb // sb,),
        in_specs=[pl.BlockSpec((n_new, sb, cdim), lambda i: (0, i, 0)),
                  pl.BlockSpec((taps - 1, sb, cdim), lambda i: (0, i, 0)), _full(w["conv_w"].shape),
                  _full(w["conv_b"].shape), _full(w["conv_ln_g"].shape), _full(w["conv_ln_b"].shape)],
        out_specs=[pl.BlockSpec((n_new, sb, cdim), lambda i: (0, i, 0)),
                   pl.BlockSpec((taps - 1, sb, cdim), lambda i: (0, i, 0))],
        out_shape=[jax.ShapeDtypeStruct((n_new, nb, cdim), bf16), jax.ShapeDtypeStruct((taps - 1, nb, cdim), f32)],
        compiler_params=_cparams("parallel"),
        name="conv_sample",
    )(g_t, buf_t, w["conv_w"], w["conv_b"], w["conv_ln_g"], w["conv_ln_b"])


def _head_norm_gate(h, g_row, so):
    mu = jnp.mean(h, -1, keepdims=True)
    hc = h - mu
    var = jnp.mean(hc * hc, -1, keepdims=True)
    return hc * lax.rsqrt(var + LN_EPS) * g_row * so


def _lane_fold(x, op):
    n = x.shape[-1]
    if n <= LANES or n % LANES:
        return x
    return functools.reduce(op, [x[:, t * LANES:(t + 1) * LANES] for t in range(n // LANES)])


def _mlstm_chunk(q, k, v, ig_col, ig_row, lf_col, lf_row, m0, cn0, last):
    c, hd = q.shape
    row = lax.broadcasted_iota(jnp.int32, (c, c), 0)
    col = lax.broadcasted_iota(jnp.int32, (c, c), 1)
    causal = col <= row
    b_col = jnp.sum(_lane_fold(jnp.where(causal, lf_row, 0.0), jnp.add), -1, keepdims=True)
    b_row = jnp.sum(jnp.where(row <= col, lf_col, 0.0), 0, keepdims=True)
    d = jnp.where(causal, b_col - b_row + ig_row, NEG_INF)
    inter = b_col + m0
    m = jnp.maximum(inter, jnp.max(_lane_fold(d, jnp.maximum), -1, keepdims=True))
    dw = jnp.exp(d - m)
    iw = jnp.exp(inter - m)
    v1 = jnp.concatenate([v, jnp.ones((c, LANES), v.dtype)], axis=-1)
    sw = _dot_nt(q, k) * dw
    both = iw * _dot(q, cn0) + _dot(sw, v1)
    den = jnp.concatenate([both[:, hd:]] * (hd // LANES), axis=-1)
    hh = both[:, :hd] / jnp.maximum(jnp.abs(den), jnp.exp(-m))
    m_last = m[last:last + 1, :]
    b_last = b_col[last:last + 1, :]
    w_last = jnp.exp(b_last - b_col + ig_col - m_last)
    decay = jnp.exp(b_last + m0 - m_last)
    kw = w_last * k.astype(f32)
    return hh, decay * cn0 + _dot(kw.T, v1), m_last


def _mlstm_prompt_kernel(qkv_ref, so_ref, gc_ref, gr_ref, ng_ref, hg_ref, c_out, n_out, m_out,
                         cn_s, m_s, *, heads, hd):
    i = pl.program_id(1)
    c = qkv_ref.shape[0]
    mdim = heads * hd

    @pl.when(i == 0)
    def _():
        cn_s[...] = jnp.zeros(cn_s.shape, f32)
        m_s[...] = jnp.zeros(m_s.shape, f32)

    gcol = gc_ref[...]
    grow = gr_ref[...]
    for h in range(heads):
        hh, cn1, m1 = _mlstm_chunk(
            qkv_ref[:, h * hd:(h + 1) * hd], qkv_ref[:, mdim + h * hd:mdim + (h + 1) * hd],
            qkv_ref[:, 2 * mdim + h * hd:2 * mdim + (h + 1) * hd],
            gcol[:, h:h + 1], grow[h:h + 1, :], _log_sigmoid(gcol[:, heads + h:heads + h + 1]),
            _log_sigmoid(grow[heads + h:heads + h + 1, :]), m_s[h][0:1, 0:1], cn_s[h], c - 1)
        cn_s[h] = cn1
        m_s[h] = jnp.broadcast_to(m1, m_s.shape[1:])
        hg_ref[:, h * hd:(h + 1) * hd] = _head_norm_gate(
            hh, ng_ref[:, h * hd:(h + 1) * hd], so_ref[:, h * hd:(h + 1) * hd]).astype(hg_ref.dtype)

    @pl.when(i == pl.num_programs(1) - 1)
    def _():
        c_out[...] = cn_s[:, :, :hd]
        n_out[...] = cn_s[:, :, hd:]
        m_out[...] = m_s[...]


def _mlstm_prompt(mqkv, so, gates, n_seq, ng, *, dims):
    T, _ = mqkv.shape
    L = T // n_seq
    heads, hd = dims["mheads"], dims["mhd"]
    mdim = heads * hd
    c = _tile(L, ML_CHUNK_PROMPT)
    nc = L // c
    ngates = gates.shape[1]
    gates_row = gates.reshape(n_seq * nc, c, ngates).transpose(0, 2, 1)
    kern = functools.partial(_mlstm_prompt_kernel, heads=heads, hd=hd)
    rowmap = lambda n, i: (n * nc + i, 0)
    return pl.pallas_call(
        kern,
        grid=(n_seq, nc),
        in_specs=[pl.BlockSpec((c, 3 * mdim), rowmap), pl.BlockSpec((c, mdim), rowmap),
                  pl.BlockSpec((c, ngates), rowmap),
                  pl.BlockSpec((None, ngates, c), lambda n, i: (n * nc + i, 0, 0)), _full(ng.shape)],
        out_specs=[pl.BlockSpec((c, mdim), rowmap),
                   pl.BlockSpec((None, heads, hd, hd), lambda n, i: (n, 0, 0, 0)),
                   pl.BlockSpec((None, heads, hd, LANES), lambda n, i: (n, 0, 0, 0)),
                   pl.BlockSpec((None, heads, 8, LANES), lambda n, i: (n, 0, 0, 0))],
        out_shape=[jax.ShapeDtypeStruct((T, mdim), bf16), jax.ShapeDtypeStruct((n_seq, heads, hd, hd), f32),
                   jax.ShapeDtypeStruct((n_seq, heads, hd, LANES), f32),
                   jax.ShapeDtypeStruct((n_seq, heads, 8, LANES), f32)],
        scratch_shapes=[pltpu.VMEM((heads, hd, hd + LANES), f32), pltpu.VMEM((heads, 8, LANES), f32)],
        compiler_params=_cparams("parallel", "arbitrary"),
        name="mlstm_prompt",
    )(mqkv, so, gates, gates_row, ng)


def _mlstm_sample_kernel(qkv_ref, so_ref, gc_ref, gr_ref, c0_ref, n0_ref, m0_ref, ng_ref,
                         hg_ref, c_out, n_out, m_out, *, heads, hd, n_new):
    sb, c, _ = qkv_ref.shape
    mdim = heads * hd
    valid_col = lax.broadcasted_iota(jnp.int32, (c, 1), 0) < n_new
    valid_row = lax.broadcasted_iota(jnp.int32, (1, c), 1) < n_new

    def seq(b, carry):
        gcol = gc_ref[b]
        grow = gr_ref[b]
        for h in range(heads):
            hh, cn1, m1 = _mlstm_chunk(
                qkv_ref[b, :, h * hd:(h + 1) * hd], qkv_ref[b, :, mdim + h * hd:mdim + (h + 1) * hd],
                qkv_ref[b, :, 2 * mdim + h * hd:2 * mdim + (h + 1) * hd],
                jnp.where(valid_col, gcol[:, h:h + 1], NEG_INF), jnp.where(valid_row, grow[h:h + 1, :], NEG_INF),
                jnp.where(valid_col, _log_sigmoid(gcol[:, heads + h:heads + h + 1]), 0.0),
                jnp.where(valid_row, _log_sigmoid(grow[heads + h:heads + h + 1, :]), 0.0),
                m0_ref[b, h:h + 1, 0:1], jnp.concatenate([c0_ref[b, h], n0_ref[b, h]], axis=-1), n_new - 1)
            c_out[b, h] = cn1[:, :hd]
            n_out[b, h] = cn1[:, hd:]
            m_out[b, h:h + 1, :] = jnp.broadcast_to(m1, (1, m_out.shape[2]))
            hg_ref[b, :, h * hd:(h + 1) * hd] = _head_norm_gate(
                hh, ng_ref[:, h * hd:(h + 1) * hd], so_ref[b, :, h * hd:(h + 1) * hd]).astype(hg_ref.dtype)
        return carry

    lax.fori_loop(0, sb, seq, 0, unroll=2)


def _mlstm_sample(mqkv_p, so_p, gcol_p, grow_p, state_c, n0, m0b, layer, ng, *, dims, n_new):
    nb, c, _ = mqkv_p.shape
    heads, hd = dims["mheads"], dims["mhd"]
    mdim = heads * hd
    ngates = gcol_p.shape[2]
    sb = _tile(nb, 8)
    kern = functools.partial(_mlstm_sample_kernel, heads=heads, hd=hd, n_new=n_new)
    seqmap = lambda i: (i, 0, 0)
    return pl.pallas_call(
        kern,
        grid=(nb // sb,),
        in_specs=[pl.BlockSpec((sb, c, 3 * mdim), seqmap), pl.BlockSpec((sb, c, mdim), seqmap),
                  pl.BlockSpec((sb, c, ngates), seqmap), pl.BlockSpec((sb, ngates, c), seqmap),
                  pl.BlockSpec((None, sb, heads, hd, hd), lambda i: (layer, i, 0, 0, 0)),
                  pl.BlockSpec((sb, heads, hd, LANES), lambda i: (i, 0, 0, 0)),
                  pl.BlockSpec((sb, heads, LANES), seqmap), _full(ng.shape)],
        out_specs=[pl.BlockSpec((sb, c, mdim), seqmap), pl.BlockSpec((sb, heads, hd, hd), lambda i: (i, 0, 0, 0)),
                   pl.BlockSpec((sb, heads, hd, LANES), lambda i: (i, 0, 0, 0)),
                   pl.BlockSpec((sb, heads, LANES), seqmap)],
        out_shape=[jax.ShapeDtypeStruct((nb, c, mdim), bf16), jax.ShapeDtypeStruct((nb, heads, hd, hd), f32),
                   jax.ShapeDtypeStruct((nb, heads, hd, LANES), f32),
                   jax.ShapeDtypeStruct((nb, heads, LANES), f32)],
        compiler_params=_cparams("parallel"),
        name="mlstm_sample",
    )(mqkv_p, so_p, gcol_p, grow_p, state_c, n0, m0b, ng)


def _merge_kernel(x_ref, va_ref, cb_ref, hc_ref, wg_ref, bg_ref, woa_ref, wob_ref, woc_ref, wout_ref,
                  lg_ref, lb_ref, wq_ref, x1_ref, qx_ref, *, alpha):
    x = x_ref[...]
    xb = x.astype(bf16)
    d = x.shape[1]
    merged = jnp.zeros(x.shape, f32)
    for k, (b_ref, w_ref) in enumerate(((va_ref, woa_ref), (cb_ref, wob_ref), (hc_ref, woc_ref))):
        gate = _sigmoid(jnp.dot(xb, wg_ref[:, k * d:(k + 1) * d], preferred_element_type=f32)
                        + bg_ref[:, k * d:(k + 1) * d])
        merged = merged + gate * _dot(b_ref[...], w_ref[...])
    mix = _dot(merged, wout_ref[...])
    x1 = _layer_norm(alpha * x + mix, lg_ref[...], lb_ref[...])
    x1_ref[...] = x1
    qx_ref[...] = _dot(x1, wq_ref[...]).astype(qx_ref.dtype)


def _merge(x, va, cb, hc, w, *, alpha):
    T, D = x.shape
    tm = _tile(T, ROW_TILE)
    row = lambda i: (i, 0)
    weights = (w["wg"], w["bg"], w["w_oa"], w["w_ob"], w["w_oc"], w["w_out"], w["ln1_g"], w["ln1_b"], w["xa_wq"])
    return pl.pallas_call(
        functools.partial(_merge_kernel, alpha=alpha),
        grid=(T // tm,),
        in_specs=[pl.BlockSpec((tm, D), row), pl.BlockSpec((tm, va.shape[1]), row),
                  pl.BlockSpec((tm, cb.shape[1]), row), pl.BlockSpec((tm, hc.shape[1]), row)]
        + [_full(a.shape) for a in weights],
        out_specs=[pl.BlockSpec((tm, D), row), pl.BlockSpec((tm, D), row)],
        out_shape=[jax.ShapeDtypeStruct((T, D), f32), jax.ShapeDtypeStruct((T, D), bf16)],
        compiler_params=_cparams("parallel"),
        name="merge",
    )(x, va, cb, hc, *weights)


def _mem_kv_kernel(mem_ref, wk_ref, wv_ref, mk_ref, mv_ref):
    mem = mem_ref[...].astype(bf16)
    mk_ref[...] = jnp.dot(mem, wk_ref[...], preferred_element_type=f32)
    mv_ref[...] = jnp.dot(mem, wv_ref[...], preferred_element_type=f32)


def _mem_kv(mem, wk, wv):
    R, D = mem.shape
    tm = _tile(R, 256)
    row = lambda i: (i, 0)
    return pl.pallas_call(
        _mem_kv_kernel,
        grid=(R // tm,),
        in_specs=[pl.BlockSpec((tm, D), row), _full(wk.shape), _full(wv.shape)],
        out_specs=[pl.BlockSpec((tm, D), row), pl.BlockSpec((tm, D), row)],
        out_shape=[jax.ShapeDtypeStruct((R, D), f32), jax.ShapeDtypeStruct((R, D), f32)],
        compiler_params=_cparams("parallel"),
        name="mem_kv",
    )(mem, wk, wv)


def _xattn_heads(q, mk, mv, heads, hd, scale):
    outs = []
    for h in range(heads):
        s = _dot_nt(q[:, h * hd:(h + 1) * hd], mk[:, h * hd:(h + 1) * hd]) * scale
        e = jnp.exp(s - jnp.max(s, -1, keepdims=True))
        p = e / jnp.sum(e, -1, keepdims=True)
        outs.append(_dot(p, mv[:, h * hd:(h + 1) * hd]))
    return outs


def _xattn_prompt_kernel(q_ref, mk_ref, mv_ref, ctx_ref, *, heads, hd, scale):
    outs = _xattn_heads(q_ref[...], mk_ref[...], mv_ref[...], heads, hd, scale)
    for h in range(heads):
        ctx_ref[:, h * hd:(h + 1) * hd] = outs[h].astype(ctx_ref.dtype)


def _xattn_prompt(qx, mk, mv, n_seq, *, dims):
    T, D = qx.shape
    L = T // n_seq
    M = mk.shape[0] // n_seq
    tm = _tile(L, 512)
    nt = L // tm
    heads, hd = dims["xheads"], dims["xhd"]
    kern = functools.partial(_xattn_prompt_kernel, heads=heads, hd=hd, scale=hd ** -0.5)
    return pl.pallas_call(
        kern,
        grid=(n_seq, nt),
        in_specs=[pl.BlockSpec((tm, D), lambda n, i: (n * nt + i, 0)), pl.BlockSpec((M, D), lambda n, i: (n, 0)),
                  pl.BlockSpec((M, D), lambda n, i: (n, 0))],
        out_specs=pl.BlockSpec((tm, D), lambda n, i: (n * nt + i, 0)),
        out_shape=jax.ShapeDtypeStruct((T, D), bf16),
        compiler_params=_cparams("parallel", "parallel"),
        name="xattn_prompt",
    )(qx, mk, mv)


def _xattn_sample_kernel(q_ref, mk_ref, mv_ref, ctx_ref, *, heads, hd, scale):
    sb = q_ref.shape[0]

    nj = hd // LANES
    mem = mk_ref.shape[1] // (nj * heads)

    def rows(j, h):
        return pl.ds(j * heads + h, mem, stride=nj * heads)

    def seq(b, carry):
        q = q_ref[b]
        for h in range(heads):
            s = _dot_nt(q[:, h * hd:h * hd + LANES], mk_ref[b, rows(0, h), :])
            for j in range(1, nj):
                s = s + _dot_nt(q[:, h * hd + j * LANES:h * hd + (j + 1) * LANES], mk_ref[b, rows(j, h), :])
            s = s * scale
            e = jnp.exp(s - jnp.max(s, -1, keepdims=True))
            p = e / jnp.sum(e, -1, keepdims=True)
            for j in range(nj):
                ctx_ref[b, :, h * hd + j * LANES:h * hd + (j + 1) * LANES] = _dot(
                    p, mv_ref[b, rows(j, h), :]).astype(ctx_ref.dtype)
        return carry

    lax.fori_loop(0, sb, seq, 0)


def _interleave_heads(mem):
    depth, n, M, heads, hd = mem.shape
    nj = hd // LANES
    return mem.reshape(depth, n, M, heads, nj, LANES).transpose(0, 1, 2, 4, 3, 5).reshape(
        depth, n, M * nj * heads, LANES)


def _xattn_sample(qx_p, mem_k, mem_v, layer, *, dims):
    nb, c, D = qx_p.shape
    R = mem_k.shape[2]
    sb = _tile(nb, 4)
    heads, hd = dims["xheads"], dims["xhd"]
    kern = functools.partial(_xattn_sample_kernel, heads=heads, hd=hd, scale=hd ** -0.5)
    memmap = lambda i: (layer, i, 0, 0)
    return pl.pallas_call(
        kern,
        grid=(nb // sb,),
        in_specs=[pl.BlockSpec((sb, c, D), lambda i: (i, 0, 0)), pl.BlockSpec((None, sb, R, LANES), memmap),
                  pl.BlockSpec((None, sb, R, LANES), memmap)],
        out_specs=pl.BlockSpec((sb, c, D), lambda i: (i, 0, 0)),
        out_shape=jax.ShapeDtypeStruct((nb, c, D), bf16),
        compiler_params=_cparams("parallel"),
        name="xattn_sample",
    )(qx_p, mem_k, mem_v)


def _post_kernel(ctx_ref, x1_ref, wo_ref, l2g_ref, l2b_ref, wup_ref, wdn_ref, l3g_ref, l3b_ref, x3_ref,
                 *, alpha, ff_chunk):
    x1 = x1_ref[...]
    x2 = _layer_norm(alpha * x1 + jnp.dot(ctx_ref[...], wo_ref[...], preferred_element_type=f32),
                     l2g_ref[...], l2b_ref[...])
    x2b = x2.astype(bf16)
    dff = wup_ref.shape[1]
    acc = jnp.zeros(x1.shape, f32)
    for c in range(dff // ff_chunk):
        hcol = jnp.maximum(jnp.dot(x2b, wup_ref[:, c * ff_chunk:(c + 1) * ff_chunk],
                                   preferred_element_type=f32), 0.0)
        acc = acc + _dot(hcol * hcol, wdn_ref[c * ff_chunk:(c + 1) * ff_chunk, :])
    x3_ref[...] = _layer_norm(alpha * x2 + acc, l3g_ref[...], l3b_ref[...])


def _post(ctx, x1, w, *, alpha):
    T, D = x1.shape
    tm = _tile(T, ROW_TILE)
    row = lambda i: (i, 0)
    weights = (w["xa_wo"], w["ln2_g"], w["ln2_b"], w["w_up"], w["w_down"], w["ln3_g"], w["ln3_b"])
    return pl.pallas_call(
        functools.partial(_post_kernel, alpha=alpha, ff_chunk=_tile(w["w_up"].shape[1], 1024)),
        grid=(T // tm,),
        in_specs=[pl.BlockSpec((tm, D), row), pl.BlockSpec((tm, D), row)] + [_full(a.shape) for a in weights],
        out_specs=pl.BlockSpec((tm, D), row),
        out_shape=jax.ShapeDtypeStruct((T, D), f32),
        compiler_params=_cparams("parallel"),
        name="post",
    )(ctx, x1, *weights)


def _rope_tables(pos, half, heads):
    inv = ROPE_BASE ** (-jnp.arange(half, dtype=f32) / half)
    ang = pos.astype(f32)[:, None] * inv[None, :]
    return jnp.tile(jnp.cos(ang), (1, heads)), jnp.tile(jnp.sin(ang), (1, heads))


def _prep_layer(l, p, dims):
    ql, kvl, half, H, nope = dims["ql"], dims["kvl"], dims["half"], dims["heads"], dims["nope"]
    cdim, mdim, mheads = dims["cdim"], dims["mdim"], dims["mheads"]
    D = dims["d"]
    w_in, b_in = p["w_in"][l], p["b_in"][l]
    o = 0

    def take(n):
        nonlocal o
        cols = (w_in[:, o:o + n], b_in[o:o + n])
        o += n
        return cols

    (w_cq, b_cq), (w_kv, b_kv), (w_kp, b_kp) = take(ql), take(kvl), take(2 * half)
    (w_cu, b_cu) = take(2 * cdim)
    (w_m, b_m) = take(4 * mdim)
    (w_if, b_if) = take(2 * mheads)
    (w_g, b_g) = take(3 * D)
    tile_h = lambda a: jnp.tile(a, (1, H)) if a.ndim == 2 else jnp.tile(a, H)
    wa = jnp.concatenate([w_cq, w_kv, tile_h(w_kp[:, :half]), tile_h(w_kp[:, half:])], 1)
    ba = jnp.concatenate([b_cq, b_kv, tile_h(b_kp[:half]), tile_h(b_kp[half:])])
    w_uq = p["w_uq"][l]
    wuq = jnp.concatenate([w_uq[:, :, :nope].reshape(ql, H * nope),
                           w_uq[:, :, nope:nope + half].reshape(ql, H * half),
                           w_uq[:, :, nope + half:].reshape(ql, H * half)], 1)
    pad_if = LANES - 2 * mheads
    return {
        "wa": wa.astype(bf16), "ba": ba[None], "qg": p["q_norm_g"][l][None], "wuq": wuq.astype(bf16),
        "kg": p["kv_norm_g"][l][None], "wuk": p["w_uk"][l].transpose(1, 2, 0).astype(bf16),
        "wc": w_cu.astype(bf16), "bc": b_cu[None], "wm": w_m.astype(bf16), "bm": b_m[None],
        "wi": jnp.pad(w_if, ((0, 0), (0, pad_if))).astype(bf16), "bi": jnp.pad(b_if, (0, pad_if))[None],
        "wuv": p["w_uv"][l].transpose(1, 0, 2).astype(bf16),
        "conv_w": p["conv_w"][l], "conv_b": p["conv_b"][l][None], "conv_ln_g": p["conv_ln_g"][l][None],
        "conv_ln_b": p["conv_ln_b"][l][None],
        "ml_norm_g": p["ml_norm_g"][l].reshape(1, mdim),
        "wg": w_g.astype(bf16), "bg": b_g[None], "w_oa": p["w_oa"][l].astype(bf16),
        "w_ob": p["w_ob"][l].astype(bf16), "w_oc": p["w_oc"][l].astype(bf16), "w_out": p["w_out"][l].astype(bf16),
        "ln1_g": p["ln1_g"][l][None], "ln1_b": p["ln1_b"][l][None], "xa_wq": p["xa_wq"][l].astype(bf16),
        "xa_wk": p["xa_wk"][l].astype(bf16), "xa_wv": p["xa_wv"][l].astype(bf16),
        "xa_wo": p["xa_wo"][l].astype(bf16), "ln2_g": p["ln2_g"][l][None], "ln2_b": p["ln2_b"][l][None],
        "w_up": p["w_up"][l].astype(bf16), "w_down": p["w_down"][l].astype(bf16),
        "ln3_g": p["ln3_g"][l][None], "ln3_b": p["ln3_b"][l][None],
    }


def _pad_tokens(a, c):
    return jnp.pad(a, ((0, 0), (0, c - a.shape[1])) + ((0, 0),) * (a.ndim - 2))


def kernel(x_prompt, x_sample, mem_prompt, cache_ckv, cache_kpe, state_conv, state_C, state_n, state_m, cache_mem_k, cache_mem_v, page_table, w_in, b_in, q_norm_g, w_uq, kv_norm_g, w_uk, w_uv, w_oa, conv_w, conv_b, conv_ln_g, conv_ln_b, w_ob, ml_norm_g, w_oc, w_out, ln1_g, ln1_b, xa_wq, xa_wk, xa_wv, xa_wo, ln2_g, ln2_b, w_up, w_down, ln3_g, ln3_b):
    params = dict(w_in=w_in, b_in=b_in, q_norm_g=q_norm_g, w_uq=w_uq, kv_norm_g=kv_norm_g, w_uk=w_uk, w_uv=w_uv,
                  w_oa=w_oa, conv_w=conv_w, conv_b=conv_b, conv_ln_g=conv_ln_g, conv_ln_b=conv_ln_b, w_ob=w_ob,
                  ml_norm_g=ml_norm_g, w_oc=w_oc, w_out=w_out, ln1_g=ln1_g, ln1_b=ln1_b, xa_wq=xa_wq, xa_wk=xa_wk,
                  xa_wv=xa_wv, xa_wo=xa_wo, ln2_g=ln2_g, ln2_b=ln2_b, w_up=w_up, w_down=w_down, ln3_g=ln3_g,
                  ln3_b=ln3_b)
    depth, D, _ = w_in.shape
    nP, Lp, _ = x_prompt.shape
    nS, Ls, _ = x_sample.shape
    H, nope = w_uk.shape[2], w_uk.shape[3]
    rope = cache_kpe.shape[-1]
    half = rope // 2
    mheads, mhd = ml_norm_g.shape[1], ml_norm_g.shape[2]
    xheads, xhd = cache_mem_k.shape[-2], cache_mem_k.shape[-1]
    M = mem_prompt.shape[1]
    dims = dict(d=D, ql=q_norm_g.shape[1], kvl=kv_norm_g.shape[1], heads=H, nope=nope, half=half,
                cdim=conv_w.shape[2], mdim=mheads * mhd, mheads=mheads, mhd=mhd, ngates=2 * mheads,
                xheads=xheads, xhd=xhd, attn_scale=(nope + rope) ** -0.5)
    assert H * half == LANES and 2 * mheads <= LANES
    kvl, mdim, cdim, taps = dims["kvl"], dims["mdim"], dims["cdim"], conv_w.shape[1]
    alpha = (2 * depth) ** 0.25
    past_len = page_table.shape[1] * cache_ckv.shape[2]
    cos_p, sin_p = _rope_tables(jnp.arange(Lp), half, H)
    cos_s, sin_s = _rope_tables(jnp.tile(past_len + jnp.arange(Ls), nS), half, H)
    C = SAMPLE_PAD

    xp = x_prompt.reshape(nP * Lp, D)
    xs = x_sample.reshape(nS * Ls, D)
    memf = mem_prompt.reshape(nP * M, D)
    cache_kpe_t = jnp.swapaxes(cache_kpe, 2, 3)
    mem_k = _interleave_heads(cache_mem_k)
    mem_v = _interleave_heads(cache_mem_v)
    outs_p = [[] for _ in range(8)]
    outs_s = [[] for _ in range(6)]
    for l in range(depth):
        w = _prep_layer(l, params, dims)
        q, kv, ckv, kpe, g, mqkv, so, gates = _inproj(xs, cos_s, sin_s, w, dims=dims)
        q_s = q.reshape(H, nS, Ls, -1).transpose(1, 0, 2, 3).reshape(nS, H * Ls, -1)
        qpe = q_s[:, :, kvl:].reshape(nS, H, Ls, 2, H, half)
        qpe = qpe[:, jnp.arange(H), :, :, jnp.arange(H), :]
        qpe = qpe.transpose(1, 0, 2, 3, 4).reshape(nS, H * Ls, rope)
        kv_new = jnp.pad(kv.reshape(nS, Ls, -1), ((0, 0), (0, LANES - Ls), (0, 0)))
        o_s = _attn_sample(page_table, q_s, qpe, kv_new, cache_ckv, cache_kpe_t, l, dims=dims, n_new=Ls)
        o_s = o_s.reshape(nS, H, Ls, kvl).transpose(1, 0, 2, 3).reshape(H, nS * Ls, kvl).astype(bf16)
        va = _attn_out(o_s, w["wuv"])
        g_t = g.reshape(nS, Ls, cdim).transpose(1, 0, 2)
        cb_t, nbuf_t = _conv_sample(g_t, state_conv[l].transpose(1, 0, 2), w)
        cb = cb_t.transpose(1, 0, 2).reshape(nS * Ls, cdim)
        gates3 = _pad_tokens(gates.reshape(nS, Ls, -1), C)
        hg, c1, n1, m1 = _mlstm_sample(
            _pad_tokens(mqkv.reshape(nS, Ls, -1), C), _pad_tokens(so.reshape(nS, Ls, -1), C), gates3,
            gates3.transpose(0, 2, 1), state_C, jnp.broadcast_to(state_n[l][..., None], (nS, mheads, mhd, LANES)),
            jnp.broadcast_to(state_m[l][:, :, None], (nS, mheads, LANES)), l, w["ml_norm_g"], dims=dims, n_new=Ls)
        hc = hg[:, :Ls].reshape(nS * Ls, mdim)
        x1, qx = _merge(xs, va, cb, hc, w, alpha=alpha)
        ctx = _xattn_sample(_pad_tokens(qx.reshape(nS, Ls, D), C), mem_k, mem_v, l, dims=dims)
        xs = _post(ctx[:, :Ls].reshape(nS * Ls, D), x1, w, alpha=alpha)
        for lst, val in zip(outs_s, (ckv.reshape(nS, Ls, kvl), kpe.reshape(nS, Ls, rope),
                                     nbuf_t.transpose(1, 0, 2), c1, n1[..., 0], m1[:, :, 0])):
            lst.append(val)
        q, kv, ckv, kpe, g, mqkv, so, gates = _inproj(xp, cos_p, sin_p, w, dims=dims)
        va = _attn_prompt(q, kv, w["wuv"], nP, dims=dims)
        cb, nbuf = _conv_prompt(g, nP, w)
        hc, c1, n1, m1 = _mlstm_prompt(mqkv, so, gates, nP, w["ml_norm_g"], dims=dims)
        x1, qx = _merge(xp, va, cb, hc, w, alpha=alpha)
        mk, mv = _mem_kv(memf, w["xa_wk"], w["xa_wv"])
        ctx = _xattn_prompt(qx, mk, mv, nP, dims=dims)
        xp = _post(ctx, x1, w, alpha=alpha)
        for lst, val in zip(outs_p, (ckv.reshape(nP, Lp, kvl), kpe.reshape(nP, Lp, rope), nbuf, c1,
                                     n1[..., 0], m1[:, :, 0, 0], mk.reshape(nP, M, xheads, xhd),
                                     mv.reshape(nP, M, xheads, xhd))):
            lst.append(val)
    return (xp.reshape(nP, Lp, D), xs.reshape(nS, Ls, D), *[jnp.stack(v) for v in outs_p],
            *[jnp.stack(v) for v in outs_s])
```

```python
import functools
import math

import jax
import jax.numpy as jnp
from jax import lax
from jax.experimental import pallas as pl
from jax.experimental.pallas import tpu as pltpu

f32 = jnp.float32
bf16 = jnp.bfloat16

LN_EPS = 1e-5
RMS_EPS = 1e-6
ROPE_BASE = 10000.0
LANES = 128
VMEM_LIMIT = 56 * 1024 * 1024
ML_CHUNK_PROMPT = 256
SAMPLE_PAD = 16
SAMPLE_PAGES_PER_STEP = 64
NEG_INF = float("-inf")
LOG2E = math.log2(math.e)
ROW_TILE = 512
ATTN_KEY_BLOCK = 256
ATTN_QUERY_TILE = 512
ATTN_TILES_PER_TRIP = 2
ATTN_HEADS_PER_GROUP = 2


def _cparams(*sem):
    return pltpu.CompilerParams(dimension_semantics=sem, vmem_limit_bytes=VMEM_LIMIT)


def _tile(n, pref):
    t = min(n, pref)
    while n % t:
        t //= 2
    return t


def _dot(a, b):
    return jnp.dot(a.astype(bf16), b.astype(bf16), preferred_element_type=f32)


def _dot_nt(a, b):
    return lax.dot_general(a.astype(bf16), b.astype(bf16), (((1,), (1,)), ((), ())),
                           preferred_element_type=f32)


def _sigmoid(x):
    return 1.0 / (1.0 + jnp.exp(-x))


def _log_sigmoid(x):
    return jnp.minimum(x, 0.0) - jnp.log(1.0 + jnp.exp(-jnp.abs(x)))


def _layer_norm(x, g, b):
    mu = jnp.mean(x, -1, keepdims=True)
    xc = x - mu
    var = jnp.mean(xc * xc, -1, keepdims=True)
    return xc * lax.rsqrt(var + LN_EPS) * g + b


def _rms_norm(x, g):
    return x * lax.rsqrt(jnp.mean(x * x, -1, keepdims=True) + RMS_EPS) * g


def _full(shape):
    nd = len(shape)
    return pl.BlockSpec(shape, lambda *_: (0,) * nd)


def _inproj_kernel(x_ref, cos_ref, sin_ref, wa_ref, ba_ref, qg_ref, wuq_ref, kg_ref, wuk_ref,
                   wc_ref, bc_ref, wm_ref, bm_ref, wi_ref, bi_ref,
                   q_ref, kv_ref, ckv_ref, kpe_ref, g_ref, mqkv_ref, so_ref, gates_ref,
                   *, ql, kvl, heads, nope, half, cdim, mdim, k_scale):
    x = x_ref[...].astype(bf16)
    cos = cos_ref[...]
    sin = sin_ref[...]
    hp = heads * half

    za = jnp.dot(x, wa_ref[...], preferred_element_type=f32) + ba_ref[...]
    cq = _rms_norm(za[:, :ql], qg_ref[...])
    ckv = _rms_norm(za[:, ql:ql + kvl], kg_ref[...])
    k1 = za[:, ql + kvl:ql + kvl + hp]
    k2 = za[:, ql + kvl + hp:]
    k1r = k1 * cos - k2 * sin
    k2r = k2 * cos + k1 * sin
    ckv_ref[...] = ckv
    kpe_ref[...] = jnp.concatenate([k1r[:, :half], k2r[:, :half]], axis=-1)
    kv_ref[:, :kvl] = ckv.astype(bf16)
    kv_ref[:, kvl:kvl + hp] = k1r.astype(bf16)
    kv_ref[:, kvl + hp:] = k2r.astype(bf16)

    qq = _dot(cq, wuq_ref[...])
    hn = heads * nope
    q1 = qq[:, hn:hn + hp]
    q2 = qq[:, hn + hp:]
    q1r = q1 * cos - q2 * sin
    q2r = q2 * cos + q1 * sin
    lane_head = lax.broadcasted_iota(jnp.int32, (1, hp), 1) // half
    for h in range(heads):
        q_lat = _dot(qq[:, h * nope:(h + 1) * nope], wuk_ref[h])
        q_ref[h, :, :kvl] = q_lat.astype(bf16)
        mine = lane_head == h
        q_ref[h, :, kvl:kvl + hp] = jnp.where(mine, q1r, 0.0).astype(bf16)
        q_ref[h, :, kvl + hp:] = jnp.where(mine, q2r, 0.0).astype(bf16)

    u = jnp.dot(x, wc_ref[...], preferred_element_type=f32) + bc_ref[...]
    g_ref[...] = u[:, :cdim] * _sigmoid(u[:, cdim:])

    zm = jnp.dot(x, wm_ref[...], preferred_element_type=f32) + bm_ref[...]
    mqkv_ref[:, :mdim] = zm[:, :mdim].astype(bf16)
    mqkv_ref[:, mdim:2 * mdim] = (zm[:, mdim:2 * mdim] * k_scale).astype(bf16)
    mqkv_ref[:, 2 * mdim:] = zm[:, 2 * mdim:3 * mdim].astype(bf16)
    so_ref[...] = _sigmoid(zm[:, 3 * mdim:])
    zi = jnp.dot(x, wi_ref[...], preferred_element_type=f32) + bi_ref[...]
    gates_ref[...] = zi[:, :gates_ref.shape[1]]


def _inproj(x, cos_t, sin_t, w, *, dims):
    T, D = x.shape
    tm = _tile(T, ROW_TILE)
    tab_blocks = cos_t.shape[0] // tm
    H, kvl, ql = dims["heads"], dims["kvl"], dims["ql"]
    hp = H * dims["half"]
    qk = kvl + 2 * hp
    cdim, mdim, ng = dims["cdim"], dims["mdim"], dims["ngates"]
    row = lambda i: (i, 0)
    tab = lambda i: (i % tab_blocks, 0)
    kern = functools.partial(_inproj_kernel, ql=ql, kvl=kvl, heads=H, nope=dims["nope"], half=dims["half"],
                             cdim=cdim, mdim=mdim, k_scale=dims["mhd"] ** -0.5)
    weights = (w["wa"], w["ba"], w["qg"], w["wuq"], w["kg"], w["wuk"], w["wc"], w["bc"], w["wm"], w["bm"],
               w["wi"], w["bi"])
    return pl.pallas_call(
        kern,
        grid=(T // tm,),
        in_specs=[pl.BlockSpec((tm, D), row), pl.BlockSpec((tm, hp), tab), pl.BlockSpec((tm, hp), tab)]
        + [_full(a.shape) for a in weights],
        out_specs=[pl.BlockSpec((H, tm, qk), lambda i: (0, i, 0)), pl.BlockSpec((tm, qk), row),
                   pl.BlockSpec((tm, kvl), row), pl.BlockSpec((tm, 2 * dims["half"]), row),
                   pl.BlockSpec((tm, cdim), row), pl.BlockSpec((tm, 3 * mdim), row),
                   pl.BlockSpec((tm, mdim), row), pl.BlockSpec((tm, ng), row)],
        out_shape=[jax.ShapeDtypeStruct((H, T, qk), bf16), jax.ShapeDtypeStruct((T, qk), bf16),
                   jax.ShapeDtypeStruct((T, kvl), f32), jax.ShapeDtypeStruct((T, 2 * dims["half"]), f32),
                   jax.ShapeDtypeStruct((T, cdim), f32), jax.ShapeDtypeStruct((T, 3 * mdim), bf16),
                   jax.ShapeDtypeStruct((T, mdim), f32), jax.ShapeDtypeStruct((T, ng), f32)],
        compiler_params=_cparams("parallel"),
        name="inproj",
    )(x, cos_t, sin_t, *weights)


def _attn_prompt_kernel(q_ref, kv_ref, wuv_ref, va_ref, m_ref, l_ref, acc_ref, *, tk, kvl, c_exp, hg):
    i = pl.program_id(1)
    heads, tq, qk = q_ref.shape
    nsub = tq // tk
    nchunk = tk // LANES
    nhg = heads // hg
    rows = hg * tk
    groups = [(h, t) for t in range(nsub) for h in range(nhg)]
    m_ref[...] = jnp.full(m_ref.shape, NEG_INF, f32)
    l_ref[...] = jnp.zeros(l_ref.shape, f32)
    acc_ref[...] = jnp.zeros(acc_ref.shape, f32)

    def run(blocks):
        items = [(j, diag, g) for (j, diag) in blocks for g in groups if diag is None or g[1] >= diag]

        def keys(j):
            return kv_ref[pl.ds(pl.multiple_of(j * tk, tk), tk), :]

        def scores(item):
            j, _, g = item
            q = q_ref[g[0] * hg:(g[0] + 1) * hg, g[1] * tk:(g[1] + 1) * tk, :]
            return _dot_nt(q.reshape(rows, qk), keys(j))

        s_next = scores(items[0])
        for idx, (j, diag, (h, t)) in enumerate(items):
            gi = t * nhg + h
            s = s_next
            if idx + 1 < len(items):
                s_next = scores(items[idx + 1])
            v = keys(j)[:, :kvl]
            if diag is not None and t == diag:
                qpos = lax.broadcasted_iota(jnp.int32, (rows, tk), 0) % tk
                kpos = lax.broadcasted_iota(jnp.int32, (rows, tk), 1)
                s = jnp.where(kpos <= qpos, s, NEG_INF)
            chunks = [s[:, c * LANES:(c + 1) * LANES] for c in range(nchunk)]
            m_prev = m_ref[gi]
            m_cur = jnp.max(functools.reduce(jnp.maximum, chunks), -1, keepdims=True)
            m_new = jnp.maximum(m_prev, m_cur)
            alpha = jnp.exp2((m_prev - m_new) * c_exp)
            ps = [jnp.exp2((ch - m_new) * c_exp) for ch in chunks]
            l_ref[gi] = alpha * l_ref[gi] + functools.reduce(jnp.add, ps)
            p = jnp.concatenate([x.astype(bf16) for x in ps], axis=-1)
            acc_ref[gi] = (jnp.concatenate([alpha] * (kvl // LANES), axis=-1) * acc_ref[gi]
                           + jnp.dot(p, v, preferred_element_type=f32))
            m_ref[gi] = m_new

    per_trip = ATTN_TILES_PER_TRIP * nsub

    def body(jj, carry):
        run([(jj * per_trip + u, None) for u in range(per_trip)])
        return carry

    lax.fori_loop(0, i // ATTN_TILES_PER_TRIP, body, 0)
    for r in range(ATTN_TILES_PER_TRIP - 1):

        @pl.when(i % ATTN_TILES_PER_TRIP > r)
        def _(r=r):
            first = ((i // ATTN_TILES_PER_TRIP) * ATTN_TILES_PER_TRIP + r) * nsub
            run([(first + u, None) for u in range(nsub)])

    run([(i * nsub + d, d) for d in range(nsub)])
    vd = wuv_ref.shape[2]
    for (h, t) in groups:
        gi = t * nhg + h
        o = (acc_ref[gi] / jnp.sum(l_ref[gi], -1, keepdims=True)).astype(bf16)
        for u in range(hg):
            head = h * hg + u
            va_ref[t * tk:(t + 1) * tk, head * vd:(head + 1) * vd] = jnp.dot(
                o[u * tk:(u + 1) * tk, :], wuv_ref[head], preferred_element_type=f32).astype(va_ref.dtype)


def _attn_prompt(q, kv, wuv, n_seq, *, dims):
    H, T, qk = q.shape
    L = T // n_seq
    kvl = dims["kvl"]
    vd = wuv.shape[2]
    tk = _tile(L, ATTN_KEY_BLOCK)
    tq = _tile(L, ATTN_QUERY_TILE)
    nq = L // tq
    hg = math.gcd(H, ATTN_HEADS_PER_GROUP)
    ngroups = (H // hg) * (tq // tk)
    kern = functools.partial(_attn_prompt_kernel, tk=tk, kvl=kvl, c_exp=dims["attn_scale"] * LOG2E, hg=hg)
    return pl.pallas_call(
        kern,
        grid=(n_seq, nq),
        in_specs=[pl.BlockSpec((H, tq, qk), lambda n, i: (0, n * nq + i, 0)),
                  pl.BlockSpec((L, qk), lambda n, i: (n, 0)), _full(wuv.shape)],
        out_specs=pl.BlockSpec((tq, H * vd), lambda n, i: (n * nq + i, 0)),
        out_shape=jax.ShapeDtypeStruct((T, H * vd), bf16),
        scratch_shapes=[pltpu.VMEM((ngroups, hg * tk, LANES), f32), pltpu.VMEM((ngroups, hg * tk, LANES), f32),
                        pltpu.VMEM((ngroups, hg * tk, kvl), f32)],
        compiler_params=_cparams("parallel", "parallel"),
        name="attn_prompt",
    )(q, kv, wuv)


def _attn_sample_kernel(pt_ref, q_ref, qpe_ref, kvn_ref, ckv_hbm, kpe_hbm, o_ref,
                        ckv_buf, kpe_buf, sem, kbuf, m_ref, l_ref, acc_ref, *, layer, pages, kvl, c_exp, n_new):
    b = pl.program_id(0)
    j = pl.program_id(1)
    steps = pl.num_programs(1)
    step = b * steps + j
    slot = step % 2
    psz = ckv_buf.shape[2]

    def page_copies(bb, jj, sl):
        out = []
        for p in range(pages):
            pg = pt_ref[bb, jj * pages + p]
            out.append(pltpu.make_async_copy(ckv_hbm.at[layer, pg], ckv_buf.at[sl, p], sem.at[sl]))
            out.append(pltpu.make_async_copy(kpe_hbm.at[layer, pg], kpe_buf.at[sl, p], sem.at[sl]))
        return out

    @pl.when(step == 0)
    def _():
        for cp in page_copies(0, 0, 0):
            cp.start()

    @pl.when(step + 1 < pl.num_programs(0) * steps)
    def _():
        wrap = j + 1 == steps
        for cp in page_copies(jnp.where(wrap, b + 1, b), jnp.where(wrap, 0, j + 1), 1 - slot):
            cp.start()

    @pl.when(j == 0)
    def _():
        m_ref[...] = jnp.full(m_ref.shape, NEG_INF, f32)
        l_ref[...] = jnp.zeros(l_ref.shape, f32)
        acc_ref[...] = jnp.zeros(acc_ref.shape, f32)

    def online(s, v):
        m_prev = m_ref[...]
        m_new = jnp.maximum(m_prev, jnp.max(s, -1, keepdims=True))
        alpha = jnp.exp2((m_prev - m_new) * c_exp)
        p = jnp.exp2((s - m_new) * c_exp)
        l_ref[...] = alpha * l_ref[...] + jnp.sum(p, -1, keepdims=True)
        acc_ref[...] = alpha * acc_ref[...] + _dot(p, v)
        m_ref[...] = m_new

    for cp in page_copies(b, j, slot):
        cp.wait()
    for p in range(pages):
        kbuf[p * psz:(p + 1) * psz, :] = ckv_buf[slot, p].astype(bf16)
    kpe_t = jnp.concatenate([kpe_buf[slot, p].astype(bf16) for p in range(pages)], axis=-1)
    k = kbuf[...]
    online(_dot_nt(q_ref[:, :kvl], k) + jnp.dot(qpe_ref[...], kpe_t, preferred_element_type=f32), k)

    @pl.when(j == steps - 1)
    def _():
        kn = kvn_ref[...]
        s = _dot_nt(q_ref[...], kn)
        tok = lax.broadcasted_iota(jnp.int32, s.shape, 0) % n_new
        col = lax.broadcasted_iota(jnp.int32, s.shape, 1)
        s = jnp.where(col <= tok, s, NEG_INF)
        online(s, kn[:, :kvl])
        o_ref[...] = acc_ref[...] / l_ref[...]


def _attn_sample(page_table, q_s, qpe_s, kv_new, cache_ckv, cache_kpe_t, layer, *, dims, n_new):
    nb, rows, qk = q_s.shape
    n_pages = page_table.shape[1]
    psz = cache_ckv.shape[2]
    kvl, rope = dims["kvl"], 2 * dims["half"]
    pages = _tile(n_pages, SAMPLE_PAGES_PER_STEP)
    steps = n_pages // pages
    kern = functools.partial(_attn_sample_kernel, layer=layer, pages=pages, kvl=kvl,
                             c_exp=dims["attn_scale"] * LOG2E, n_new=n_new)
    grid_spec = pltpu.PrefetchScalarGridSpec(
        num_scalar_prefetch=1,
        grid=(nb, steps),
        in_specs=[pl.BlockSpec((None, rows, qk), lambda b, j, pt: (b, 0, 0)),
                  pl.BlockSpec((None, rows, rope), lambda b, j, pt: (b, 0, 0)),
                  pl.BlockSpec((None, LANES, qk), lambda b, j, pt: (b, 0, 0)),
                  pl.BlockSpec(memory_space=pl.ANY), pl.BlockSpec(memory_space=pl.ANY)],
        out_specs=pl.BlockSpec((None, rows, kvl), lambda b, j, pt: (b, 0, 0)),
        scratch_shapes=[pltpu.VMEM((2, pages, psz, kvl), f32), pltpu.VMEM((2, pages, rope, psz), f32),
                        pltpu.SemaphoreType.DMA((2,)), pltpu.VMEM((pages * psz, kvl), bf16),
                        pltpu.VMEM((rows, 1), f32), pltpu.VMEM((rows, 1), f32), pltpu.VMEM((rows, kvl), f32)],
    )
    return pl.pallas_call(
        kern,
        grid_spec=grid_spec,
        out_shape=jax.ShapeDtypeStruct((nb, rows, kvl), f32),
        compiler_params=_cparams("arbitrary", "arbitrary"),
        name="attn_sample",
    )(page_table, q_s, qpe_s, kv_new, cache_ckv, cache_kpe_t)


def _attn_out_kernel(o_ref, wuv_ref, va_ref):
    heads = o_ref.shape[0]
    vd = wuv_ref.shape[2]
    for h in range(heads):
        va_ref[:, h * vd:(h + 1) * vd] = jnp.dot(o_ref[h], wuv_ref[h],
                                                 preferred_element_type=f32).astype(va_ref.dtype)


def _attn_out(o, wuv):
    H, T, kvl = o.shape
    vd = wuv.shape[2]
    tm = _tile(T, 512)
    return pl.pallas_call(
        _attn_out_kernel,
        grid=(T // tm,),
        in_specs=[pl.BlockSpec((H, tm, kvl), lambda i: (0, i, 0)), _full(wuv.shape)],
        out_specs=pl.BlockSpec((tm, H * vd), lambda i: (i, 0)),
        out_shape=jax.ShapeDtypeStruct((T, H * vd), bf16),
        compiler_params=_cparams("parallel"),
        name="attn_out",
    )(o, wuv)


def _conv_prompt_kernel(g_ref, cw_ref, cb_ref, lg_ref, lb_ref, act_ref, buf_ref, xp_ref, sh_ref, *, taps, rc):
    i = pl.program_id(1)
    tm, cdim = g_ref.shape
    head = xp_ref.shape[0] - tm
    keep = taps - 1

    @pl.when(i == 0)
    def _():
        xp_ref[:head, :] = jnp.zeros((head, cdim), f32)

    xp_ref[head:, :] = g_ref[...]
    w = cw_ref[...]
    lead = head - keep
    span = sh_ref.shape[1]
    for r in range(1, 8):
        sh_ref[r - 1] = xp_ref[r:r + span, :]
    for c in range(tm // rc):
        acc = jnp.zeros((rc, cdim), f32)
        for t in range(taps):
            a, r = divmod(lead + t, 8)
            rows = slice(c * rc + 8 * a, c * rc + 8 * a + rc)
            acc = acc + (xp_ref[rows, :] if r == 0 else sh_ref[r - 1, rows, :]) * w[t:t + 1, :]
        y = _layer_norm(acc + cb_ref[...], lg_ref[...], lb_ref[...])
        act_ref[c * rc:(c + 1) * rc, :] = (y * _sigmoid(y)).astype(act_ref.dtype)

    @pl.when(i == pl.num_programs(1) - 1)
    def _():
        buf_ref[...] = xp_ref[head + tm - keep:, :]

    xp_ref[:head, :] = xp_ref[tm:, :]


def _conv_prompt(g, n_seq, w):
    T, cdim = g.shape
    L = T // n_seq
    taps = w["conv_w"].shape[0]
    tm = _tile(L, 256)
    nt = L // tm
    head = 32
    assert taps - 1 <= head <= tm
    kern = functools.partial(_conv_prompt_kernel, taps=taps, rc=_tile(tm, 32))
    return pl.pallas_call(
        kern,
        grid=(n_seq, nt),
        in_specs=[pl.BlockSpec((tm, cdim), lambda n, i: (n * nt + i, 0)), _full(w["conv_w"].shape),
                  _full(w["conv_b"].shape), _full(w["conv_ln_g"].shape), _full(w["conv_ln_b"].shape)],
        out_specs=[pl.BlockSpec((tm, cdim), lambda n, i: (n * nt + i, 0)),
                   pl.BlockSpec((None, taps - 1, cdim), lambda n, i: (n, 0, 0))],
        out_shape=[jax.ShapeDtypeStruct((T, cdim), bf16), jax.ShapeDtypeStruct((n_seq, taps - 1, cdim), f32)],
        scratch_shapes=[pltpu.VMEM((head + tm, cdim), f32), pltpu.VMEM((7, head + tm - 8, cdim), f32)],
        compiler_params=_cparams("parallel", "arbitrary"),
        name="conv_prompt",
    )(g, w["conv_w"], w["conv_b"], w["conv_ln_g"], w["conv_ln_b"])


def _conv_sample_kernel(g_ref, buf_ref, cw_ref, cb_ref, lg_ref, lb_ref, act_ref, nbuf_ref, *, taps):
    n_new = g_ref.shape[0]
    keep = taps - 1
    w = cw_ref[...]

    def xp(r):
        return buf_ref[r] if r < keep else g_ref[r - keep]

    for t in range(n_new):
        acc = xp(t) * w[0:1, :]
        for k in range(1, taps):
            acc = acc + xp(t + k) * w[k:k + 1, :]
        y = _layer_norm(acc + cb_ref[...], lg_ref[...], lb_ref[...])
        act_ref[t] = (y * _sigmoid(y)).astype(act_ref.dtype)
    for r in range(keep):
        nbuf_ref[r] = xp(r + n_new)


def _conv_sample(g_t, buf_t, w):
    n_new, nb, cdim = g_t.shape
    taps = w["conv_w"].shape[0]
    sb = _tile(nb, 32)
    kern = functools.partial(_conv_sample_kernel, taps=taps)
    return pl.pallas_call(
        kern,
        grid=(nb // sb,),
        in_specs=[pl.BlockSpec((n_new, sb, cdim), lambda i: (0, i, 0)),
                  pl.BlockSpec((taps - 1, sb, cdim), lambda i: (0, i, 0)), _full(w["conv_w"].shape),
                  _full(w["conv_b"].shape), _full(w["conv_ln_g"].shape), _full(w["conv_ln_b"].shape)],
        out_specs=[pl.BlockSpec((n_new, sb, cdim), lambda i: (0, i, 0)),
                   pl.BlockSpec((taps - 1, sb, cdim), lambda i: (0, i, 0))],
        out_shape=[jax.ShapeDtypeStruct((n_new, nb, cdim), bf16), jax.ShapeDtypeStruct((taps - 1, nb, cdim), f32)],
        compiler_params=_cparams("parallel"),
        name="conv_sample",
    )(g_t, buf_t, w["conv_w"], w["conv_b"], w["conv_ln_g"], w["conv_ln_b"])


def _head_norm_gate(h, g_row, so):
    mu = jnp.mean(h, -1, keepdims=True)
    hc = h - mu
    var = jnp.mean(hc * hc, -1, keepdims=True)
    return hc * lax.rsqrt(var + LN_EPS) * g_row * so


def _lane_fold(x, op):
    n = x.shape[-1]
    if n <= LANES or n % LANES:
        return x
    return functools.reduce(op, [x[:, t * LANES:(t + 1) * LANES] for t in range(n // LANES)])


def _mlstm_chunk(q, k, v, ig_col, ig_row, lf_col, lf_row, m0, cn0, last):
    c, hd = q.shape
    row = lax.broadcasted_iota(jnp.int32, (c, c), 0)
    col = lax.broadcasted_iota(jnp.int32, (c, c), 1)
    causal = col <= row
    b_col = jnp.sum(_lane_fold(jnp.where(causal, lf_row, 0.0), jnp.add), -1, keepdims=True)
    b_row = jnp.sum(jnp.where(row <= col, lf_col, 0.0), 0, keepdims=True)
    d = jnp.where(causal, b_col - b_row + ig_row, NEG_INF)
    inter = b_col + m0
    m = jnp.maximum(inter, jnp.max(_lane_fold(d, jnp.maximum), -1, keepdims=True))
    dw = jnp.exp(d - m)
    iw = jnp.exp(inter - m)
    v1 = jnp.concatenate([v, jnp.ones((c, LANES), v.dtype)], axis=-1)
    sw = _dot_nt(q, k) * dw
    both = iw * _dot(q, cn0) + _dot(sw, v1)
    den = jnp.concatenate([both[:, hd:]] * (hd // LANES), axis=-1)
    hh = both[:, :hd] / jnp.maximum(jnp.abs(den), jnp.exp(-m))
    m_last = m[last:last + 1, :]
    b_last = b_col[last:last + 1, :]
    w_last = jnp.exp(b_last - b_col + ig_col - m_last)
    decay = jnp.exp(b_last + m0 - m_last)
    kw = w_last * k.astype(f32)
    return hh, decay * cn0 + _dot(kw.T, v1), m_last


def _mlstm_prompt_kernel(qkv_ref, so_ref, gc_ref, gr_ref, ng_ref, hg_ref, c_out, n_out, m_out,
                         cn_s, m_s, *, heads, hd):
    i = pl.program_id(1)
    c = qkv_ref.shape[0]
    mdim = heads * hd

    @pl.when(i == 0)
    def _():
        cn_s[...] = jnp.zeros(cn_s.shape, f32)
        m_s[...] = jnp.zeros(m_s.shape, f32)

    gcol = gc_ref[...]
    grow = gr_ref[...]
    for h in range(heads):
        hh, cn1, m1 = _mlstm_chunk(
            qkv_ref[:, h * hd:(h + 1) * hd], qkv_ref[:, mdim + h * hd:mdim + (h + 1) * hd],
            qkv_ref[:, 2 * mdim + h * hd:2 * mdim + (h + 1) * hd],
            gcol[:, h:h + 1], grow[h:h + 1, :], _log_sigmoid(gcol[:, heads + h:heads + h + 1]),
            _log_sigmoid(grow[heads + h:heads + h + 1, :]), m_s[h][0:1, 0:1], cn_s[h], c - 1)
        cn_s[h] = cn1
        m_s[h] = jnp.broadcast_to(m1, m_s.shape[1:])
        hg_ref[:, h * hd:(h + 1) * hd] = _head_norm_gate(
            hh, ng_ref[:, h * hd:(h + 1) * hd], so_ref[:, h * hd:(h + 1) * hd]).astype(hg_ref.dtype)

    @pl.when(i == pl.num_programs(1) - 1)
    def _():
        c_out[...] = cn_s[:, :, :hd]
        n_out[...] = cn_s[:, :, hd:]
        m_out[...] = m_s[...]


def _mlstm_prompt(mqkv, so, gates, n_seq, ng, *, dims):
    T, _ = mqkv.shape
    L = T // n_seq
    heads, hd = dims["mheads"], dims["mhd"]
    mdim = heads * hd
    c = _tile(L, ML_CHUNK_PROMPT)
    nc = L // c
    ngates = gates.shape[1]
    gates_row = gates.reshape(n_seq * nc, c, ngates).transpose(0, 2, 1)
    kern = functools.partial(_mlstm_prompt_kernel, heads=heads, hd=hd)
    rowmap = lambda n, i: (n * nc + i, 0)
    return pl.pallas_call(
        kern,
        grid=(n_seq, nc),
        in_specs=[pl.BlockSpec((c, 3 * mdim), rowmap), pl.BlockSpec((c, mdim), rowmap),
                  pl.BlockSpec((c, ngates), rowmap),
                  pl.BlockSpec((None, ngates, c), lambda n, i: (n * nc + i, 0, 0)), _full(ng.shape)],
        out_specs=[pl.BlockSpec((c, mdim), rowmap),
                   pl.BlockSpec((None, heads, hd, hd), lambda n, i: (n, 0, 0, 0)),
                   pl.BlockSpec((None, heads, hd, LANES), lambda n, i: (n, 0, 0, 0)),
                   pl.BlockSpec((None, heads, 8, LANES), lambda n, i: (n, 0, 0, 0))],
        out_shape=[jax.ShapeDtypeStruct((T, mdim), bf16), jax.ShapeDtypeStruct((n_seq, heads, hd, hd), f32),
                   jax.ShapeDtypeStruct((n_seq, heads, hd, LANES), f32),
                   jax.ShapeDtypeStruct((n_seq, heads, 8, LANES), f32)],
        scratch_shapes=[pltpu.VMEM((heads, hd, hd + LANES), f32), pltpu.VMEM((heads, 8, LANES), f32)],
        compiler_params=_cparams("parallel", "arbitrary"),
        name="mlstm_prompt",
    )(mqkv, so, gates, gates_row, ng)


def _mlstm_sample_kernel(qkv_ref, so_ref, gc_ref, gr_ref, c0_ref, n0_ref, m0_ref, ng_ref,
                         hg_ref, c_out, n_out, m_out, *, heads, hd, n_new):
    sb, c, _ = qkv_ref.shape
    mdim = heads * hd
    valid_col = lax.broadcasted_iota(jnp.int32, (c, 1), 0) < n_new
    valid_row = lax.broadcasted_iota(jnp.int32, (1, c), 1) < n_new

    def seq(b, carry):
        gcol = gc_ref[b]
        grow = gr_ref[b]
        for h in range(heads):
            hh, cn1, m1 = _mlstm_chunk(
                qkv_ref[b, :, h * hd:(h + 1) * hd], qkv_ref[b, :, mdim + h * hd:mdim + (h + 1) * hd],
                qkv_ref[b, :, 2 * mdim + h * hd:2 * mdim + (h + 1) * hd],
                jnp.where(valid_col, gcol[:, h:h + 1], NEG_INF), jnp.where(valid_row, grow[h:h + 1, :], NEG_INF),
                jnp.where(valid_col, _log_sigmoid(gcol[:, heads + h:heads + h + 1]), 0.0),
                jnp.where(valid_row, _log_sigmoid(grow[heads + h:heads + h + 1, :]), 0.0),
                m0_ref[b, h:h + 1, 0:1], jnp.concatenate([c0_ref[b, h], n0_ref[b, h]], axis=-1), n_new - 1)
            c_out[b, h] = cn1[:, :hd]
            n_out[b, h] = cn1[:, hd:]
            m_out[b, h:h + 1, :] = jnp.broadcast_to(m1, (1, m_out.shape[2]))
            hg_ref[b, :, h * hd:(h + 1) * hd] = _head_norm_gate(
                hh, ng_ref[:, h * hd:(h + 1) * hd], so_ref[b, :, h * hd:(h + 1) * hd]).astype(hg_ref.dtype)
        return carry

    lax.fori_loop(0, sb, seq, 0, unroll=2)


def _mlstm_sample(mqkv_p, so_p, gcol_p, grow_p, state_c, n0, m0b, layer, ng, *, dims, n_new):
    nb, c, _ = mqkv_p.shape
    heads, hd = dims["mheads"], dims["mhd"]
    mdim = heads * hd
    ngates = gcol_p.shape[2]
    sb = _tile(nb, 8)
    kern = functools.partial(_mlstm_sample_kernel, heads=heads, hd=hd, n_new=n_new)
    seqmap = lambda i: (i, 0, 0)
    return pl.pallas_call(
        kern,
        grid=(nb // sb,),
        in_specs=[pl.BlockSpec((sb, c, 3 * mdim), seqmap), pl.BlockSpec((sb, c, mdim), seqmap),
                  pl.BlockSpec((sb, c, ngates), seqmap), pl.BlockSpec((sb, ngates, c), seqmap),
                  pl.BlockSpec((None, sb, heads, hd, hd), lambda i: (layer, i, 0, 0, 0)),
                  pl.BlockSpec((sb, heads, hd, LANES), lambda i: (i, 0, 0, 0)),
                  pl.BlockSpec((sb, heads, LANES), seqmap), _full(ng.shape)],
        out_specs=[pl.BlockSpec((sb, c, mdim), seqmap), pl.BlockSpec((sb, heads, hd, hd), lambda i: (i, 0, 0, 0)),
                   pl.BlockSpec((sb, heads, hd, LANES), lambda i: (i, 0, 0, 0)),
                   pl.BlockSpec((sb, heads, LANES), seqmap)],
        out_shape=[jax.ShapeDtypeStruct((nb, c, mdim), bf16), jax.ShapeDtypeStruct((nb, heads, hd, hd), f32),
                   jax.ShapeDtypeStruct((nb, heads, hd, LANES), f32),
                   jax.ShapeDtypeStruct((nb, heads, LANES), f32)],
        compiler_params=_cparams("parallel"),
        name="mlstm_sample",
    )(mqkv_p, so_p, gcol_p, grow_p, state_c, n0, m0b, ng)


def _merge_kernel(x_ref, va_ref, cb_ref, hc_ref, wg_ref, bg_ref, woa_ref, wob_ref, woc_ref, wout_ref,
                  lg_ref, lb_ref, wq_ref, x1_ref, qx_ref, *, alpha):
    x = x_ref[...]
    xb = x.astype(bf16)
    d = x.shape[1]
    merged = jnp.zeros(x.shape, f32)
    for k, (b_ref, w_ref) in enumerate(((va_ref, woa_ref), (cb_ref, wob_ref), (hc_ref, woc_ref))):
        gate = _sigmoid(jnp.dot(xb, wg_ref[:, k * d:(k + 1) * d], preferred_element_type=f32)
                        + bg_ref[:, k * d:(k + 1) * d])
        merged = merged + gate * _dot(b_ref[...], w_ref[...])
    mix = _dot(merged, wout_ref[...])
    x1 = _layer_norm(alpha * x + mix, lg_ref[...], lb_ref[...])
    x1_ref[...] = x1
    qx_ref[...] = _dot(x1, wq_ref[...]).astype(qx_ref.dtype)


def _merge(x, va, cb, hc, w, *, alpha):
    T, D = x.shape
    tm = _tile(T, ROW_TILE)
    row = lambda i: (i, 0)
    weights = (w["wg"], w["bg"], w["w_oa"], w["w_ob"], w["w_oc"], w["w_out"], w["ln1_g"], w["ln1_b"], w["xa_wq"])
    return pl.pallas_call(
        functools.partial(_merge_kernel, alpha=alpha),
        grid=(T // tm,),
        in_specs=[pl.BlockSpec((tm, D), row), pl.BlockSpec((tm, va.shape[1]), row),
                  pl.BlockSpec((tm, cb.shape[1]), row), pl.BlockSpec((tm, hc.shape[1]), row)]
        + [_full(a.shape) for a in weights],
        out_specs=[pl.BlockSpec((tm, D), row), pl.BlockSpec((tm, D), row)],
        out_shape=[jax.ShapeDtypeStruct((T, D), f32), jax.ShapeDtypeStruct((T, D), bf16)],
        compiler_params=_cparams("parallel"),
        name="merge",
    )(x, va, cb, hc, *weights)


def _mem_kv_kernel(mem_ref, wk_ref, wv_ref, mk_ref, mv_ref):
    mem = mem_ref[...].astype(bf16)
    mk_ref[...] = jnp.dot(mem, wk_ref[...], preferred_element_type=f32)
    mv_ref[...] = jnp.dot(mem, wv_ref[...], preferred_element_type=f32)


def _mem_kv(mem, wk, wv):
    R, D = mem.shape
    tm = _tile(R, 256)
    row = lambda i: (i, 0)
    return pl.pallas_call(
        _mem_kv_kernel,
        grid=(R // tm,),
        in_specs=[pl.BlockSpec((tm, D), row), _full(wk.shape), _full(wv.shape)],
        out_specs=[pl.BlockSpec((tm, D), row), pl.BlockSpec((tm, D), row)],
        out_shape=[jax.ShapeDtypeStruct((R, D), f32), jax.ShapeDtypeStruct((R, D), f32)],
        compiler_params=_cparams("parallel"),
        name="mem_kv",
    )(mem, wk, wv)


def _xattn_heads(q, mk, mv, heads, hd, scale):
    outs = []
    for h in range(heads):
        s = _dot_nt(q[:, h * hd:(h + 1) * hd], mk[:, h * hd:(h + 1) * hd]) * scale
        e = jnp.exp(s - jnp.max(s, -1, keepdims=True))
        p = e / jnp.sum(e, -1, keepdims=True)
        outs.append(_dot(p, mv[:, h * hd:(h + 1) * hd]))
    return outs


def _xattn_prompt_kernel(q_ref, mk_ref, mv_ref, ctx_ref, *, heads, hd, scale):
    outs = _xattn_heads(q_ref[...], mk_ref[...], mv_ref[...], heads, hd, scale)
    for h in range(heads):
        ctx_ref[:, h * hd:(h + 1) * hd] = outs[h].astype(ctx_ref.dtype)


def _xattn_prompt(qx, mk, mv, n_seq, *, dims):
    T, D = qx.shape
    L = T // n_seq
    M = mk.shape[0] // n_seq
    tm = _tile(L, 512)
    nt = L // tm
    heads, hd = dims["xheads"], dims["xhd"]
    kern = functools.partial(_xattn_prompt_kernel, heads=heads, hd=hd, scale=hd ** -0.5)
    return pl.pallas_call(
        kern,
        grid=(n_seq, nt),
        in_specs=[pl.BlockSpec((tm, D), lambda n, i: (n * nt + i, 0)), pl.BlockSpec((M, D), lambda n, i: (n, 0)),
                  pl.BlockSpec((M, D), lambda n, i: (n, 0))],
        out_specs=pl.BlockSpec((tm, D), lambda n, i: (n * nt + i, 0)),
        out_shape=jax.ShapeDtypeStruct((T, D), bf16),
        compiler_params=_cparams("parallel", "parallel"),
        name="xattn_prompt",
    )(qx, mk, mv)


def _xattn_sample_kernel(q_ref, mk_ref, mv_ref, ctx_ref, *, heads, hd, scale):
    sb = q_ref.shape[0]

    nj = hd // LANES
    mem = mk_ref.shape[1] // (nj * heads)

    def rows(j, h):
        return pl.ds(j * heads + h, mem, stride=nj * heads)

    def seq(b, carry):
        q = q_ref[b]
        for h in range(heads):
            s = _dot_nt(q[:, h * hd:h * hd + LANES], mk_ref[b, rows(0, h), :])
            for j in range(1, nj):
                s = s + _dot_nt(q[:, h * hd + j * LANES:h * hd + (j + 1) * LANES], mk_ref[b, rows(j, h), :])
            s = s * scale
            e = jnp.exp(s - jnp.max(s, -1, keepdims=True))
            p = e / jnp.sum(e, -1, keepdims=True)
            for j in range(nj):
                ctx_ref[b, :, h * hd + j * LANES:h * hd + (j + 1) * LANES] = _dot(
                    p, mv_ref[b, rows(j, h), :]).astype(ctx_ref.dtype)
        return carry

    lax.fori_loop(0, sb, seq, 0)


def _interleave_heads(mem):
    depth, n, M, heads, hd = mem.shape
    nj = hd // LANES
    return mem.reshape(depth, n, M, heads, nj, LANES).transpose(0, 1, 2, 4, 3, 5).reshape(
        depth, n, M * nj * heads, LANES)


def _xattn_sample(qx_p, mem_k, mem_v, layer, *, dims):
    nb, c, D = qx_p.shape
    R = mem_k.shape[2]
    sb = _tile(nb, 4)
    heads, hd = dims["xheads"], dims["xhd"]
    kern = functools.partial(_xattn_sample_kernel, heads=heads, hd=hd, scale=hd ** -0.5)
    memmap = lambda i: (layer, i, 0, 0)
    return pl.pallas_call(
        kern,
        grid=(nb // sb,),
        in_specs=[pl.BlockSpec((sb, c, D), lambda i: (i, 0, 0)), pl.BlockSpec((None, sb, R, LANES), memmap),
                  pl.BlockSpec((None, sb, R, LANES), memmap)],
        out_specs=pl.BlockSpec((sb, c, D), lambda i: (i, 0, 0)),
        out_shape=jax.ShapeDtypeStruct((nb, c, D), bf16),
        compiler_params=_cparams("parallel"),
        name="xattn_sample",
    )(qx_p, mem_k, mem_v)


def _post_kernel(ctx_ref, x1_ref, wo_ref, l2g_ref, l2b_ref, wup_ref, wdn_ref, l3g_ref, l3b_ref, x3_ref,
                 *, alpha, ff_chunk):
    x1 = x1_ref[...]
    x2 = _layer_norm(alpha * x1 + jnp.dot(ctx_ref[...], wo_ref[...], preferred_element_type=f32),
                     l2g_ref[...], l2b_ref[...])
    x2b = x2.astype(bf16)
    dff = wup_ref.shape[1]
    acc = jnp.zeros(x1.shape, f32)
    for c in range(dff // ff_chunk):
        hcol = jnp.maximum(jnp.dot(x2b, wup_ref[:, c * ff_chunk:(c + 1) * ff_chunk],
                                   preferred_element_type=f32), 0.0)
        acc = acc + _dot(hcol * hcol, wdn_ref[c * ff_chunk:(c + 1) * ff_chunk, :])
    x3_ref[...] = _layer_norm(alpha * x2 + acc, l3g_ref[...], l3b_ref[...])


def _post(ctx, x1, w, *, alpha):
    T, D = x1.shape
    tm = _tile(T, ROW_TILE)
    row = lambda i: (i, 0)
    weights = (w["xa_wo"], w["ln2_g"], w["ln2_b"], w["w_up"], w["w_down"], w["ln3_g"], w["ln3_b"])
    return pl.pallas_call(
        functools.partial(_post_kernel, alpha=alpha, ff_chunk=_tile(w["w_up"].shape[1], 1024)),
        grid=(T // tm,),
        in_specs=[pl.BlockSpec((tm, D), row), pl.BlockSpec((tm, D), row)] + [_full(a.shape) for a in weights],
        out_specs=pl.BlockSpec((tm, D), row),
        out_shape=jax.ShapeDtypeStruct((T, D), f32),
        compiler_params=_cparams("parallel"),
        name="post",
    )(ctx, x1, *weights)


def _rope_tables(pos, half, heads):
    inv = ROPE_BASE ** (-jnp.arange(half, dtype=f32) / half)
    ang = pos.astype(f32)[:, None] * inv[None, :]
    return jnp.tile(jnp.cos(ang), (1, heads)), jnp.tile(jnp.sin(ang), (1, heads))


def _prep_layer(l, p, dims):
    ql, kvl, half, H, nope = dims["ql"], dims["kvl"], dims["half"], dims["heads"], dims["nope"]
    cdim, mdim, mheads = dims["cdim"], dims["mdim"], dims["mheads"]
    D = dims["d"]
    w_in, b_in = p["w_in"][l], p["b_in"][l]
    o = 0

    def take(n):
        nonlocal o
        cols = (w_in[:, o:o + n], b_in[o:o + n])
        o += n
        return cols

    (w_cq, b_cq), (w_kv, b_kv), (w_kp, b_kp) = take(ql), take(kvl), take(2 * half)
    (w_cu, b_cu) = take(2 * cdim)
    (w_m, b_m) = take(4 * mdim)
    (w_if, b_if) = take(2 * mheads)
    (w_g, b_g) = take(3 * D)
    tile_h = lambda a: jnp.tile(a, (1, H)) if a.ndim == 2 else jnp.tile(a, H)
    wa = jnp.concatenate([w_cq, w_kv, tile_h(w_kp[:, :half]), tile_h(w_kp[:, half:])], 1)
    ba = jnp.concatenate([b_cq, b_kv, tile_h(b_kp[:half]), tile_h(b_kp[half:])])
    w_uq = p["w_uq"][l]
    wuq = jnp.concatenate([w_uq[:, :, :nope].reshape(ql, H * nope),
                           w_uq[:, :, nope:nope + half].reshape(ql, H * half),
                           w_uq[:, :, nope + half:].reshape(ql, H * half)], 1)
    pad_if = LANES - 2 * mheads
    return {
        "wa": wa.astype(bf16), "ba": ba[None], "qg": p["q_norm_g"][l][None], "wuq": wuq.astype(bf16),
        "kg": p["kv_norm_g"][l][None], "wuk": p["w_uk"][l].transpose(1, 2, 0).astype(bf16),
        "wc": w_cu.astype(bf16), "bc": b_cu[None], "wm": w_m.astype(bf16), "bm": b_m[None],
        "wi": jnp.pad(w_if, ((0, 0), (0, pad_if))).astype(bf16), "bi": jnp.pad(b_if, (0, pad_if))[None],
        "wuv": p["w_uv"][l].transpose(1, 0, 2).astype(bf16),
        "conv_w": p["conv_w"][l], "conv_b": p["conv_b"][l][None], "conv_ln_g": p["conv_ln_g"][l][None],
        "conv_ln_b": p["conv_ln_b"][l][None],
        "ml_norm_g": p["ml_norm_g"][l].reshape(1, mdim),
        "wg": w_g.astype(bf16), "bg": b_g[None], "w_oa": p["w_oa"][l].astype(bf16),
        "w_ob": p["w_ob"][l].astype(bf16), "w_oc": p["w_oc"][l].astype(bf16), "w_out": p["w_out"][l].astype(bf16),
        "ln1_g": p["ln1_g"][l][None], "ln1_b": p["ln1_b"][l][None], "xa_wq": p["xa_wq"][l].astype(bf16),
        "xa_wk": p["xa_wk"][l].astype(bf16), "xa_wv": p["xa_wv"][l].astype(bf16),
        "xa_wo": p["xa_wo"][l].astype(bf16), "ln2_g": p["ln2_g"][l][None], "ln2_b": p["ln2_b"][l][None],
        "w_up": p["w_up"][l].astype(bf16), "w_down": p["w_down"][l].astype(bf16),
        "ln3_g": p["ln3_g"][l][None], "ln3_b": p["ln3_b"][l][None],
    }


def _pad_tokens(a, c):
    return jnp.pad(a, ((0, 0), (0, c - a.shape[1])) + ((0, 0),) * (a.ndim - 2))


def kernel(x_prompt, x_sample, mem_prompt, cache_ckv, cache_kpe, state_conv, state_C, state_n, state_m, cache_mem_k, cache_mem_v, page_table, w_in, b_in, q_norm_g, w_uq, kv_norm_g, w_uk, w_uv, w_oa, conv_w, conv_b, conv_ln_g, conv_ln_b, w_ob, ml_norm_g, w_oc, w_out, ln1_g, ln1_b, xa_wq, xa_wk, xa_wv, xa_wo, ln2_g, ln2_b, w_up, w_down, ln3_g, ln3_b):
    params = dict(w_in=w_in, b_in=b_in, q_norm_g=q_norm_g, w_uq=w_uq, kv_norm_g=kv_norm_g, w_uk=w_uk, w_uv=w_uv,
                  w_oa=w_oa, conv_w=conv_w, conv_b=conv_b, conv_ln_g=conv_ln_g, conv_ln_b=conv_ln_b, w_ob=w_ob,
                  ml_norm_g=ml_norm_g, w_oc=w_oc, w_out=w_out, ln1_g=ln1_g, ln1_b=ln1_b, xa_wq=xa_wq, xa_wk=xa_wk,
                  xa_wv=xa_wv, xa_wo=xa_wo, ln2_g=ln2_g, ln2_b=ln2_b, w_up=w_up, w_down=w_down, ln3_g=ln3_g,
                  ln3_b=ln3_b)
    depth, D, _ = w_in.shape
    nP, Lp, _ = x_prompt.shape
    nS, Ls, _ = x_sample.shape
    H, nope = w_uk.shape[2], w_uk.shape[3]
    rope = cache_kpe.shape[-1]
    half = rope // 2
    mheads, mhd = ml_norm_g.shape[1], ml_norm_g.shape[2]
    xheads, xhd = cache_mem_k.shape[-2], cache_mem_k.shape[-1]
    M = mem_prompt.shape[1]
    dims = dict(d=D, ql=q_norm_g.shape[1], kvl=kv_norm_g.shape[1], heads=H, nope=nope, half=half,
                cdim=conv_w.shape[2], mdim=mheads * mhd, mheads=mheads, mhd=mhd, ngates=2 * mheads,
                xheads=xheads, xhd=xhd, attn_scale=(nope + rope) ** -0.5)
    assert H * half == LANES and 2 * mheads <= LANES
    kvl, mdim, cdim, taps = dims["kvl"], dims["mdim"], dims["cdim"], conv_w.shape[1]
    alpha = (2 * depth) ** 0.25
    past_len = page_table.shape[1] * cache_ckv.shape[2]
    cos_p, sin_p = _rope_tables(jnp.arange(Lp), half, H)
    cos_s, sin_s = _rope_tables(jnp.tile(past_len + jnp.arange(Ls), nS), half, H)
    C = SAMPLE_PAD

    xp = x_prompt.reshape(nP * Lp, D)
    xs = x_sample.reshape(nS * Ls, D)
    memf = mem_prompt.reshape(nP * M, D)
    cache_kpe_t = jnp.swapaxes(cache_kpe, 2, 3)
    mem_k = _interleave_heads(cache_mem_k)
    mem_v = _interleave_heads(cache_mem_v)
    outs_p = [[] for _ in range(8)]
    outs_s = [[] for _ in range(6)]
    for l in range(depth):
        w = _prep_layer(l, params, dims)
        q, kv, ckv, kpe, g, mqkv, so, gates = _inproj(xs, cos_s, sin_s, w, dims=dims)
        q_s = q.reshape(H, nS, Ls, -1).transpose(1, 0, 2, 3).reshape(nS, H * Ls, -1)
        qpe = q_s[:, :, kvl:].reshape(nS, H, Ls, 2, H, half)
        qpe = qpe[:, jnp.arange(H), :, :, jnp.arange(H), :]
        qpe = qpe.transpose(1, 0, 2, 3, 4).reshape(nS, H * Ls, rope)
        kv_new = jnp.pad(kv.reshape(nS, Ls, -1), ((0, 0), (0, LANES - Ls), (0, 0)))
        o_s = _attn_sample(page_table, q_s, qpe, kv_new, cache_ckv, cache_kpe_t, l, dims=dims, n_new=Ls)
        o_s = o_s.reshape(nS, H, Ls, kvl).transpose(1, 0, 2, 3).reshape(H, nS * Ls, kvl).astype(bf16)
        va = _attn_out(o_s, w["wuv"])
        g_t = g.reshape(nS, Ls, cdim).transpose(1, 0, 2)
        cb_t, nbuf_t = _conv_sample(g_t, state_conv[l].transpose(1, 0, 2), w)
        cb = cb_t.transpose(1, 0, 2).reshape(nS * Ls, cdim)
        gates3 = _pad_tokens(gates.reshape(nS, Ls, -1), C)
        hg, c1, n1, m1 = _mlstm_sample(
            _pad_tokens(mqkv.reshape(nS, Ls, -1), C), _pad_tokens(so.reshape(nS, Ls, -1), C), gates3,
            gates3.transpose(0, 2, 1), state_C, jnp.broadcast_to(state_n[l][..., None], (nS, mheads, mhd, LANES)),
            jnp.broadcast_to(state_m[l][:, :, None], (nS, mheads, LANES)), l, w["ml_norm_g"], dims=dims, n_new=Ls)
        hc = hg[:, :Ls].reshape(nS * Ls, mdim)
        x1, qx = _merge(xs, va, cb, hc, w, alpha=alpha)
        ctx = _xattn_sample(_pad_tokens(qx.reshape(nS, Ls, D), C), mem_k, mem_v, l, dims=dims)
        xs = _post(ctx[:, :Ls].reshape(nS * Ls, D), x1, w, alpha=alpha)
        for lst, val in zip(outs_s, (ckv.reshape(nS, Ls, kvl), kpe.reshape(nS, Ls, rope),
                                     nbuf_t.transpose(1, 0, 2), c1, n1[..., 0], m1[:, :, 0])):
            lst.append(val)
        q, kv, ckv, kpe, g, mqkv, so, gates = _inproj(xp, cos_p, sin_p, w, dims=dims)
        va = _attn_prompt(q, kv, w["wuv"], nP, dims=dims)
        cb, nbuf = _conv_prompt(g, nP, w)
        hc, c1, n1, m1 = _mlstm_prompt(mqkv, so, gates, nP, w["ml_norm_g"], dims=dims)
        x1, qx = _merge(xp, va, cb, hc, w, alpha=alpha)
        mk, mv = _mem_kv(memf, w["xa_wk"], w["xa_wv"])
        ctx = _xattn_prompt(qx, mk, mv, nP, dims=dims)
        xp = _post(ctx, x1, w, alpha=alpha)
        for lst, val in zip(outs_p, (ckv.reshape(nP, Lp, kvl), kpe.reshape(nP, Lp, rope), nbuf, c1,
                                     n1[..., 0], m1[:, :, 0, 0], mk.reshape(nP, M, xheads, xhd),
                                     mv.reshape(nP, M, xheads, xhd))):
            lst.append(val)
    return (xp.reshape(nP, Lp, D), xs.reshape(nS, Ls, D), *[jnp.stack(v) for v in outs_p],
            *[jnp.stack(v) for v in outs_s])
```
